```python
import jax, jax.numpy as jnp
from jax import lax
import numpy as np

D_MODEL = 1024
BATCH = 4
SEQ = 8192
DEPTH = 1
DEC_BATCH = 32
DEC_SEQ = 1
PAST_LEN = 16384
PAGE_SIZE = 128

HEAD_DIM = 64
N_FOX_HEADS = D_MODEL // (2 * HEAD_DIM)
N_NSA_HEADS = D_MODEL // (2 * HEAD_DIM)
N_NSA_KV = 2
NSA_GROUP = N_NSA_HEADS // N_NSA_KV
FOX_WIDTH = N_FOX_HEADS * HEAD_DIM
NSA_WIDTH = N_NSA_HEADS * HEAD_DIM
KV_WIDTH = N_NSA_KV * HEAD_DIM
PROJ_SIZES = (FOX_WIDTH, FOX_WIDTH, FOX_WIDTH, N_FOX_HEADS, NSA_WIDTH) + (KV_WIDTH,) * 6 + (3 * N_NSA_HEADS,)
PROJ_DIM = 3 * FOX_WIDTH + N_FOX_HEADS + NSA_WIDTH + 6 * KV_WIDTH + 3 * N_NSA_HEADS
CMP_BLOCK = 32
CMP_STRIDE = 16
SEL_BLOCK = 64
N_SELECT = 16
WINDOW = 512
Q_BLOCK = 128
FORCE_SCORE = 1.0e4
N_EXPERTS = 32
TOP_K = 4
D_EXPERT = D_MODEL
SWIGLU_LIMIT = 7.0
SWIGLU_ALPHA = 1.702
MOE_BLOCK = 256
RMS_EPS = 1e-6
FORGET_BIAS_INIT = 2.0

kernel_name = 'fox_nsa_hymba_moe_step'


def _rmsnorm(x, g):
    xf = x.astype(jnp.float32)
    y = xf * lax.rsqrt(jnp.mean(xf * xf, axis=-1, keepdims=True) + RMS_EPS)
    return (y * g.astype(jnp.float32)).astype(x.dtype)


def _masked_softmax(s, mask):
    s = jnp.where(mask, s, -jnp.inf)
    m = jnp.max(s, axis=-1, keepdims=True)
    e = jnp.exp(s - jnp.where(jnp.isfinite(m), m, 0.0))
    return e / jnp.maximum(jnp.sum(e, axis=-1, keepdims=True), jnp.finfo(jnp.float32).tiny)


def _alibi_slopes(n):
    return jnp.exp2(-8.0 * jnp.arange(1, n + 1, dtype=jnp.float32) / n)


def _project(x, g_attn, w_in, b_fox_f, b_nsa_gate, g_q_fox, g_k_fox, g_q_nsa, g_k_sel, g_k_win):
    B, S, _ = x.shape
    z = _rmsnorm(x, g_attn) @ w_in
    splits = np.cumsum(PROJ_SIZES)[:-1].tolist()
    fq, fk, fv, ff, nq, ck, cv, sk, sv, wk, wv, ng = jnp.split(z, splits, axis=-1)
    heads = lambda t: t.reshape(B, S, -1, HEAD_DIM)
    return {
        'fox_q': _rmsnorm(heads(fq), g_q_fox),
        'fox_k': _rmsnorm(heads(fk), g_k_fox),
        'fox_v': heads(fv),
        'fox_logf': jax.nn.log_sigmoid((ff + b_fox_f).astype(jnp.float32)),
        'nsa_q': _rmsnorm(heads(nq), g_q_nsa),
        'cmp_k': heads(ck), 'cmp_v': heads(cv),
        'sel_k': _rmsnorm(heads(sk), g_k_sel), 'sel_v': heads(sv),
        'win_k': _rmsnorm(heads(wk), g_k_win), 'win_v': heads(wv),
        'nsa_gate': jax.nn.sigmoid((ng + b_nsa_gate).astype(jnp.float32)).reshape(B, S, N_NSA_HEADS, 3),
    }


def _compress(rows, pe, w):
    B, T = rows.shape[:2]
    nc = (T - CMP_BLOCK) // CMP_STRIDE + 1
    idx = jnp.arange(nc)[:, None] * CMP_STRIDE + jnp.arange(CMP_BLOCK)[None, :]
    blk = rows[:, idx] + pe[:, None, :]
    blk = jnp.moveaxis(blk, 3, 2).reshape(B, nc, N_NSA_KV, CMP_BLOCK * HEAD_DIM)
    return blk @ w, idx[:, -1]


def _cmp_kv(cmp_k, cmp_v, pe_cmp_k, pe_cmp_v, w_cmp_k, w_cmp_v, g_k_cmp):
    kc, cend = _compress(cmp_k, pe_cmp_k, w_cmp_k)
    vc, _ = _compress(cmp_v, pe_cmp_v, w_cmp_v)
    return _rmsnorm(kc, g_k_cmp), vc, cend


def _overlap(nc, nsb):
    c0 = jnp.arange(nc)[:, None] * CMP_STRIDE
    s0 = jnp.arange(nsb)[None, :] * SEL_BLOCK
    return ((c0 < s0 + SEL_BLOCK) & (c0 + CMP_BLOCK > s0)).astype(jnp.float32)


def _sel_blocks(rows):
    B, T = rows.shape[:2]
    nsb = -(-T // SEL_BLOCK)
    rows = jnp.pad(rows, ((0, 0), (0, nsb * SEL_BLOCK - T), (0, 0), (0, 0)))
    return rows.reshape(B, nsb, SEL_BLOCK, N_NSA_KV, HEAD_DIM).transpose(0, 3, 1, 2, 4)


def _fox_block(q, cq, tq, k, v, ck, tk):
    s = jnp.einsum('bqhd,bkhd->bhqk', q, k).astype(jnp.float32) * HEAD_DIM ** -0.5
    s = s + jnp.moveaxis(cq, 2, 1)[..., None] - jnp.moveaxis(ck, 2, 1)[:, :, None, :]
    p = _masked_softmax(s, tk[None, :] <= tq[:, None])
    return jnp.einsum('bhqk,bkhd->bqhd', p.astype(v.dtype), v)


def _nsa_block(q, tq, gate, kc, vc, cend, overlap, ksb, vsb, kw, vw, tw, slopes):
    B, Qb = q.shape[:2]
    qg = q.reshape(B, Qb, N_NSA_KV, NSA_GROUP, HEAD_DIM)
    sl = slopes.reshape(1, N_NSA_KV, NSA_GROUP, 1, 1)
    scale = HEAD_DIM ** -0.5
    d_c = (tq[:, None] - cend[None, :]).astype(jnp.float32)
    s_c = jnp.einsum('bqgjd,bngd->bgjqn', qg, kc).astype(jnp.float32) * scale - sl * d_c
    p_c = _masked_softmax(s_c, d_c >= 0)
    o_c = jnp.einsum('bgjqn,bngd->bqgjd', p_c.astype(vc.dtype), vc)
    nsb = ksb.shape[2]
    jb = jnp.arange(nsb)[None, :]
    cur = (tq // SEL_BLOCK)[:, None]
    imp = jnp.einsum('bgjqn,nm->bgqm', p_c, overlap)
    imp = jnp.where((jb == 0) | (jb == cur) | (jb == cur - 1), FORCE_SCORE, imp)
    imp = jnp.where(jb * SEL_BLOCK <= tq[:, None], imp, -jnp.inf)
    _, sel = lax.top_k(imp, min(N_SELECT, nsb))
    n = sel.shape[-1]
    take = jax.vmap(jax.vmap(lambda blk, ix: blk[ix]))
    k_s = take(ksb, sel)
    v_s = take(vsb, sel).reshape(B, N_NSA_KV, Qb, n * SEL_BLOCK, HEAD_DIM)
    pos = sel[..., None] * SEL_BLOCK + jnp.arange(SEL_BLOCK)
    d_s = (tq[:, None, None] - pos).astype(jnp.float32)[:, :, None].reshape(B, N_NSA_KV, 1, Qb, n * SEL_BLOCK)
    s_s = jnp.einsum('bqgjd,bgqnld->bgjqnl', qg, k_s).astype(jnp.float32)
    s_s = s_s.reshape(B, N_NSA_KV, NSA_GROUP, Qb, n * SEL_BLOCK) * scale - sl * d_s
    p_s = _masked_softmax(s_s, d_s >= 0)
    o_s = jnp.einsum('bgjqm,bgqmd->bqgjd', p_s.astype(v_s.dtype), v_s)
    d_w = (tq[:, None] - tw[None, :]).astype(jnp.float32)
    ok_w = (d_w >= 0) & (d_w <= WINDOW) & (tw[None, :] >= 0)
    s_w = jnp.einsum('bqgjd,bkgd->bgjqk', qg, kw).astype(jnp.float32) * scale - sl * d_w
    p_w = _masked_softmax(s_w, ok_w)
    o_w = jnp.einsum('bgjqk,bkgd->bqgjd', p_w.astype(vw.dtype), vw)
    g = gate.reshape(B, Qb, N_NSA_KV, NSA_GROUP, 3).astype(o_c.dtype)
    o = g[..., 0:1] * o_c + g[..., 1:2] * o_s + g[..., 2:3] * o_w
    return o.reshape(B, Qb, N_NSA_HEADS, HEAD_DIM)


def _mix_prompt(p, slopes, pe_cmp_k, pe_cmp_v, w_cmp_k, w_cmp_v, g_k_cmp):
    B, S = p['fox_q'].shape[:2]
    nb = S // Q_BLOCK
    t_all = jnp.arange(S)
    c_fox = jnp.cumsum(p['fox_logf'], axis=1)
    kc, vc, cend = _cmp_kv(p['cmp_k'], p['cmp_v'], pe_cmp_k, pe_cmp_v, w_cmp_k, w_cmp_v, g_k_cmp)
    ksb, vsb = _sel_blocks(p['sel_k']), _sel_blocks(p['sel_v'])
    overlap = _overlap(kc.shape[1], ksb.shape[2])
    pad = ((0, 0), (WINDOW, 0), (0, 0), (0, 0))
    kw_pad, vw_pad = jnp.pad(p['win_k'], pad), jnp.pad(p['win_v'], pad)

    def blocks(t):
        return jnp.moveaxis(t.reshape(B, nb, Q_BLOCK, *t.shape[2:]), 1, 0)

    def body(args):
        i, qf, cq, qn, gate = args
        q0 = i * Q_BLOCK
        tq = q0 + jnp.arange(Q_BLOCK)
        o_fox = _fox_block(qf, cq, tq, p['fox_k'], p['fox_v'], c_fox, t_all)
        kw = lax.dynamic_slice_in_dim(kw_pad, q0, WINDOW + Q_BLOCK, axis=1)
        vw = lax.dynamic_slice_in_dim(vw_pad, q0, WINDOW + Q_BLOCK, axis=1)
        tw = q0 - WINDOW + jnp.arange(WINDOW + Q_BLOCK)
        o_nsa = _nsa_block(qn, tq, gate, kc, vc, cend, overlap, ksb, vsb, kw, vw, tw, slopes)
        return o_fox, o_nsa

    o_fox, o_nsa = lax.map(body, (jnp.arange(nb), blocks(p['fox_q']), blocks(c_fox),
                                  blocks(p['nsa_q']), blocks(p['nsa_gate'])))
    unblock = lambda t: jnp.moveaxis(t, 0, 1).reshape(B, S, -1)
    return jnp.concatenate([unblock(o_fox), unblock(o_nsa)], axis=-1)


def _mix_sample(p, c_fox_k, c_fox_v, c_fox_logf, c_cmp_k, c_cmp_v, c_sel_k, c_sel_v, w_k, w_v,
                page_table, slopes, pe_cmp_k, pe_cmp_v, w_cmp_k, w_cmp_v, g_k_cmp):
    B, DS = p['fox_q'].shape[:2]
    past = page_table.shape[1] * c_fox_k.shape[1]

    def paged(cache):
        return cache[page_table].reshape(B, past, *cache.shape[2:])

    cat = lambda cache, new: jnp.concatenate([paged(cache), new], axis=1)
    t_all = jnp.arange(past + DS)
    tq = past + jnp.arange(DS)
    c_fox = jnp.cumsum(jnp.concatenate([paged(c_fox_logf).astype(jnp.float32), p['fox_logf']], axis=1), axis=1)
    o_fox = _fox_block(p['fox_q'], c_fox[:, past:], tq, cat(c_fox_k, p['fox_k']), cat(c_fox_v, p['fox_v']),
                       c_fox, t_all)
    kc, vc, cend = _cmp_kv(cat(c_cmp_k, p['cmp_k']), cat(c_cmp_v, p['cmp_v']),
                           pe_cmp_k, pe_cmp_v, w_cmp_k, w_cmp_v, g_k_cmp)
    ksb, vsb = _sel_blocks(cat(c_sel_k, p['sel_k'])), _sel_blocks(cat(c_sel_v, p['sel_v']))
    overlap = _overlap(kc.shape[1], ksb.shape[2])
    nbuf = w_k.shape[1]
    kw = jnp.concatenate([w_k, p['win_k']], axis=1)
    vw = jnp.concatenate([w_v, p['win_v']], axis=1)
    tw = past - nbuf + jnp.arange(nbuf + DS)
    o_nsa = _nsa_block(p['nsa_q'], tq, p['nsa_gate'], kc, vc, cend, overlap, ksb, vsb, kw, vw, tw, slopes)
    mix = jnp.concatenate([o_fox.reshape(B, DS, -1), o_nsa.reshape(B, DS, -1)], axis=-1)
    return mix, kw[:, -nbuf:], vw[:, -nbuf:]


def _moe(h, w_router, b_router, w_gate_up, b_gate_up, w_down, b_down):
    N, D = h.shape
    logits = (h @ w_router).astype(jnp.float32) + b_router.astype(jnp.float32)
    top_v, top_i = lax.top_k(logits, TOP_K)
    gates = jax.nn.softmax(top_v, axis=-1)
    flat_e = top_i.reshape(-1)
    flat_t = jnp.repeat(jnp.arange(N, dtype=jnp.int32), TOP_K)
    order = jnp.argsort(flat_e)
    e_sorted = flat_e[order]
    counts = jnp.bincount(flat_e, length=N_EXPERTS)
    padded = (counts + MOE_BLOCK - 1) // MOE_BLOCK * MOE_BLOCK
    start = jnp.cumsum(counts) - counts
    pend = jnp.cumsum(padded)
    dest = (pend - padded)[e_sorted] + jnp.arange(N * TOP_K) - start[e_sorted]
    n_slots = (-(-N * TOP_K // MOE_BLOCK) + N_EXPERTS) * MOE_BLOCK
    nblk = n_slots // MOE_BLOCK
    slot_tok = jnp.full((n_slots,), N, jnp.int32).at[dest].set(flat_t[order])
    slot_gate = jnp.zeros((n_slots,), jnp.float32).at[dest].set(gates.reshape(-1)[order])
    blk_expert = jnp.clip(jnp.searchsorted(pend, jnp.arange(nblk) * MOE_BLOCK, side='right'), 0, N_EXPERTS - 1)
    h_pad = jnp.concatenate([h, jnp.zeros((1, D), h.dtype)], axis=0)

    def expert_block(args):
        e, toks = args
        gu = h_pad[toks] @ w_gate_up[e] + b_gate_up[e]
        g = jnp.minimum(gu[:, 0::2], SWIGLU_LIMIT)
        u = jnp.clip(gu[:, 1::2], -SWIGLU_LIMIT, SWIGLU_LIMIT)
        act = (u + 1.0) * g * jax.nn.sigmoid(SWIGLU_ALPHA * g)
        return act @ w_down[e] + b_down[e]

    out = lax.map(expert_block, (blk_expert, slot_tok.reshape(nblk, MOE_BLOCK))).reshape(n_slots, D)
    out = out * slot_gate[:, None].astype(out.dtype)
    return jnp.zeros((N + 1, D), out.dtype).at[slot_tok].add(out)[:N]


def _tail(x, mix, w_out, g_ffn, w_router, b_router, w_gate_up, b_gate_up, w_down, b_down):
    B, S, D = x.shape
    x = x + mix @ w_out
    y = _moe(_rmsnorm(x, g_ffn).reshape(B * S, D), w_router, b_router, w_gate_up, b_gate_up, w_down, b_down)
    return x + y.reshape(B, S, D)


def setup_inputs(seed: int = 0) -> dict:
    key = jax.random.key(seed)
    ks = iter(jax.random.split(key, 48))
    nrm = lambda shape, scale: jax.random.normal(next(ks), shape, jnp.float32) * scale
    L = DEPTH
    n_pages = PAST_LEN // PAGE_SIZE
    n_used = DEC_BATCH * n_pages
    n_pool = n_used + max(1, n_used // 4)
    w_buf = min(WINDOW, PAST_LEN)
    gain = lambda n: 1.0 + nrm((L, n), 0.02)
    fox_pool = (L, n_pool, PAGE_SIZE, N_FOX_HEADS, HEAD_DIM)
    nsa_pool = (L, n_pool, PAGE_SIZE, N_NSA_KV, HEAD_DIM)
    win_buf = (L, DEC_BATCH, w_buf, N_NSA_KV, HEAD_DIM)
    return {
        'x_prompt': nrm((BATCH, SEQ, D_MODEL), 1.0),
        'x_sample': nrm((DEC_BATCH, DEC_SEQ, D_MODEL), 1.0),
        'cache_fox_k': nrm(fox_pool, 1.0),
        'cache_fox_v': nrm(fox_pool, 1.0),
        'cache_fox_logf': jax.nn.log_sigmoid(FORGET_BIAS_INIT + nrm((L, n_pool, PAGE_SIZE, N_FOX_HEADS), 1.0)),
        'cache_cmp_k': nrm(nsa_pool, 1.0),
        'cache_cmp_v': nrm(nsa_pool, 1.0),
        'cache_sel_k': nrm(nsa_pool, 1.0),
        'cache_sel_v': nrm(nsa_pool, 1.0),
        'state_win_k': nrm(win_buf, 1.0),
        'state_win_v': nrm(win_buf, 1.0),
        'page_table': jax.random.permutation(next(ks), n_pool)[:n_used].reshape(DEC_BATCH, n_pages).astype(jnp.int32),
        'g_attn': gain(D_MODEL),
        'w_in': nrm((L, D_MODEL, PROJ_DIM), D_MODEL ** -0.5),
        'b_fox_f': FORGET_BIAS_INIT + nrm((L, N_FOX_HEADS), 0.1),
        'b_nsa_gate': nrm((L, 3 * N_NSA_HEADS), 0.1),
        'g_q_fox': gain(HEAD_DIM),
        'g_k_fox': gain(HEAD_DIM),
        'g_q_nsa': gain(HEAD_DIM),
        'g_k_sel': gain(HEAD_DIM),
        'g_k_win': gain(HEAD_DIM),
        'g_k_cmp': gain(HEAD_DIM),
        'pe_cmp_k': nrm((L, CMP_BLOCK, HEAD_DIM), 0.02),
        'pe_cmp_v': nrm((L, CMP_BLOCK, HEAD_DIM), 0.02),
        'w_cmp_k': nrm((L, CMP_BLOCK * HEAD_DIM, HEAD_DIM), (CMP_BLOCK * HEAD_DIM) ** -0.5),
        'w_cmp_v': nrm((L, CMP_BLOCK * HEAD_DIM, HEAD_DIM), (CMP_BLOCK * HEAD_DIM) ** -0.5),
        'w_out': nrm((L, FOX_WIDTH + NSA_WIDTH, D_MODEL), (FOX_WIDTH + NSA_WIDTH) ** -0.5),
        'g_ffn': gain(D_MODEL),
        'w_router': nrm((L, D_MODEL, N_EXPERTS), D_MODEL ** -0.5),
        'b_router': nrm((L, N_EXPERTS), 0.01),
        'w_gate_up': nrm((L, N_EXPERTS, D_MODEL, 2 * D_EXPERT), D_MODEL ** -0.5),
        'b_gate_up': nrm((L, N_EXPERTS, 2 * D_EXPERT), 0.01),
        'w_down': nrm((L, N_EXPERTS, D_EXPERT, D_MODEL), D_EXPERT ** -0.5),
        'b_down': nrm((L, N_EXPERTS, D_MODEL), 0.01),
    }


def reference(x_prompt, x_sample, cache_fox_k, cache_fox_v, cache_fox_logf, cache_cmp_k, cache_cmp_v,
              cache_sel_k, cache_sel_v, state_win_k, state_win_v, page_table,
              g_attn, w_in, b_fox_f, b_nsa_gate, g_q_fox, g_k_fox, g_q_nsa, g_k_sel, g_k_win, g_k_cmp,
              pe_cmp_k, pe_cmp_v, w_cmp_k, w_cmp_v, w_out, g_ffn, w_router, b_router,
              w_gate_up, b_gate_up, w_down, b_down):
    slopes = _alibi_slopes(N_NSA_HEADS)
    xp, xs = x_prompt, x_sample
    acc_p = [[] for _ in range(9)]
    acc_s = [[] for _ in range(9)]
    for l in range(DEPTH):
        proj_w = (g_attn[l], w_in[l], b_fox_f[l], b_nsa_gate[l], g_q_fox[l], g_k_fox[l],
                  g_q_nsa[l], g_k_sel[l], g_k_win[l])
        cmp_w = (pe_cmp_k[l], pe_cmp_v[l], w_cmp_k[l], w_cmp_v[l], g_k_cmp[l])
        tail_w = (w_out[l], g_ffn[l], w_router[l], b_router[l], w_gate_up[l], b_gate_up[l], w_down[l], b_down[l])
        pp = _project(xp, *proj_w)
        xp = _tail(xp, _mix_prompt(pp, slopes, *cmp_w), *tail_w)
        wp = min(WINDOW, pp['win_k'].shape[1])
        rows_p = (pp['fox_k'], pp['fox_v'], pp['fox_logf'], pp['cmp_k'], pp['cmp_v'], pp['sel_k'], pp['sel_v'],
                  pp['win_k'][:, -wp:], pp['win_v'][:, -wp:])
        ps = _project(xs, *proj_w)
        mix_s, win_k_s, win_v_s = _mix_sample(ps, cache_fox_k[l], cache_fox_v[l], cache_fox_logf[l],
                                              cache_cmp_k[l], cache_cmp_v[l], cache_sel_k[l], cache_sel_v[l],
                                              state_win_k[l], state_win_v[l], page_table, slopes, *cmp_w)
        xs = _tail(xs, mix_s, *tail_w)
        rows_s = (ps['fox_k'], ps['fox_v'], ps['fox_logf'], ps['cmp_k'], ps['cmp_v'], ps['sel_k'], ps['sel_v'],
                  win_k_s, win_v_s)
        for a, r in zip(acc_p, rows_p):
            a.append(r)
        for a, r in zip(acc_s, rows_s):
            a.append(r)
    p_fox_k, p_fox_v, p_fox_logf, p_cmp_k, p_cmp_v, p_sel_k, p_sel_v, p_win_k, p_win_v = [jnp.stack(a) for a in acc_p]
    s_fox_k, s_fox_v, s_fox_logf, s_cmp_k, s_cmp_v, s_sel_k, s_sel_v, s_win_k, s_win_v = [jnp.stack(a) for a in acc_s]
    return (xp, xs,
            p_fox_k, p_fox_v, p_fox_logf, p_cmp_k, p_cmp_v, p_sel_k, p_sel_v, p_win_k, p_win_v,
            s_fox_k, s_fox_v, s_fox_logf, s_cmp_k, s_cmp_v, s_sel_k, s_sel_v, s_win_k, s_win_v)
```

```python
import functools

import numpy as np
import jax
import jax.numpy as jnp
from jax import lax
from jax.experimental import pallas as pl
from jax.experimental.pallas import tpu as pltpu

F32 = jnp.float32
BF16 = jnp.bfloat16

HEAD_DIM = 64
LANES = 128
N_FOX_HEADS = 8
N_NSA_HEADS = 8
N_NSA_KV = 2
NSA_GROUP = N_NSA_HEADS // N_NSA_KV
CMP_BLOCK = 32
CMP_STRIDE = 16
SEL_BLOCK = 64
N_SELECT = 16
WINDOW = 512
FORCE_SCORE = 1.0e4
N_EXPERTS = 32
TOP_K = 4
SWIGLU_LIMIT = 7.0
SWIGLU_ALPHA = 1.702
RMS_EPS = 1e-6
PAGE_SIZE = 128
NEG = -1.0e30
TINY = float(np.finfo(np.float32).tiny)
VMEM_LIMIT = 56 * 1024 * 1024


def _cparams(sem):
    return pltpu.CompilerParams(dimension_semantics=sem, vmem_limit_bytes=VMEM_LIMIT)


def _dot(a, b):
    return jnp.dot(a, b, preferred_element_type=F32)


def _dot_nt(a, b):
    return lax.dot_general(a, b, (((1,), (1,)), ((), ())), preferred_element_type=F32)


def _dot_tn(a, b):
    return lax.dot_general(a, b, (((0,), (0,)), ((), ())), preferred_element_type=F32)


def _split2(x):
    hi = x.astype(BF16)
    lo = (x - hi.astype(F32)).astype(BF16)
    return hi, lo


def _split3(x):
    hi = x.astype(BF16)
    r = x - hi.astype(F32)
    mid = r.astype(BF16)
    lo = (r - mid.astype(F32)).astype(BF16)
    return hi, mid, lo


def _dot_x2(x, m):
    hi, lo = _split2(x)
    return _dot(hi, m) + _dot(lo, m)


def _head_rms(t, gain, hm):
    outs = []
    for c in range(t.shape[1] // LANES):
        tc = t[:, c * LANES:(c + 1) * LANES]
        ms = _dot_x2(tc * tc, hm)
        outs.append(tc * lax.rsqrt(ms + RMS_EPS) * gain)
    return outs[0] if len(outs) == 1 else jnp.concatenate(outs, axis=1)


def _head_mean_matrix():
    i = np.arange(LANES)
    return jnp.asarray((i[:, None] // HEAD_DIM == i[None, :] // HEAD_DIM) / HEAD_DIM, BF16)


_FQ0, _NQ0, _FK0, _FV0, _KV0, _MISC0, _PROJ_COLS = 0, 1024, 2048, 2560, 3072, 3840, 3968


def _proj_weights(w_in, b_fox_f, b_nsa_gate):
    fw = N_FOX_HEADS * HEAD_DIM
    kvw = N_NSA_KV * HEAD_DIM
    o = 0
    fq = w_in[:, o:o + fw]; o += fw
    fk = w_in[:, o:o + fw]; o += fw
    fv = w_in[:, o:o + fw]; o += fw
    ff = w_in[:, o:o + N_FOX_HEADS]; o += N_FOX_HEADS
    nq = w_in[:, o:o + fw]; o += fw
    kv = w_in[:, o:o + 6 * kvw]; o += 6 * kvw
    ng = w_in[:, o:o + 3 * N_NSA_HEADS]
    d = w_in.shape[0]
    z64 = jnp.zeros((d, HEAD_DIM), w_in.dtype)
    fq_x, nq_x = [], []
    for h in range(N_FOX_HEADS):
        qh = fq[:, h * HEAD_DIM:(h + 1) * HEAD_DIM]
        fq_x += [qh, z64] if h % 2 == 0 else [z64, qh]
    for h in range(N_NSA_HEADS):
        qh = nq[:, h * HEAD_DIM:(h + 1) * HEAD_DIM]
        fq_pos = h // NSA_GROUP
        nq_x += [qh, z64] if fq_pos == 0 else [z64, qh]
    misc = jnp.concatenate([ff, ng, jnp.zeros((d, LANES - N_FOX_HEADS - 3 * N_NSA_HEADS), w_in.dtype)], axis=1)
    w = jnp.concatenate(fq_x + nq_x + [fk, fv, kv, misc], axis=1).astype(BF16)
    bias = jnp.concatenate([b_fox_f, b_nsa_gate,
                            jnp.zeros((LANES - N_FOX_HEADS - 3 * N_NSA_HEADS,), F32)]).reshape(1, LANES)
    return w, bias


def _proj_kernel(x_ref, g_ref, w_ref, bias_ref, gains_ref, hm_ref,
                 fq_ref, nq_ref, fk_ref, fv_ref, ck_ref, cv_ref, sk_ref, sv_ref, wk_ref, wv_ref, misc_ref):
    x = x_ref[...]
    ms = jnp.mean(x * x, axis=-1, keepdims=True)
    xn = (x * lax.rsqrt(ms + RMS_EPS)) * g_ref[...]
    z = _dot(xn.astype(BF16), w_ref[...])
    hm = hm_ref[...]
    scale = HEAD_DIM ** -0.5
    fq_ref[...] = (_head_rms(z[:, _FQ0:_NQ0], gains_ref[0:1, :], hm) * scale).astype(BF16)
    nq_ref[...] = (_head_rms(z[:, _NQ0:_FK0], gains_ref[1:2, :], hm) * scale).astype(BF16)
    fk_ref[...] = _head_rms(z[:, _FK0:_FV0], gains_ref[2:3, :], hm)
    fv_ref[...] = z[:, _FV0:_KV0]
    kv = z[:, _KV0:_MISC0]
    ck_ref[...] = kv[:, 0:128]
    cv_ref[...] = kv[:, 128:256]
    sk_ref[...] = _head_rms(kv[:, 256:384], gains_ref[3:4, :], hm)
    sv_ref[...] = kv[:, 384:512]
    wk_ref[...] = _head_rms(kv[:, 512:640], gains_ref[4:5, :], hm)
    wv_ref[...] = kv[:, 640:768]
    zb = z[:, _MISC0:_PROJ_COLS] + bias_ref[...]
    soft = jnp.log1p(jnp.exp(-jnp.abs(zb)))
    logsig = jnp.minimum(zb, 0.0) - soft
    lane = lax.broadcasted_iota(jnp.int32, zb.shape, 1)
    misc_ref[...] = jnp.where(lane < N_FOX_HEADS, logsig, jnp.exp(logsig))


def _project(x2d, g_attn, w, bias, gains, hm, tm):
    n, d = x2d.shape
    row = lambda c: pl.BlockSpec((tm, c), lambda i: (i, 0))
    full = lambda a: pl.BlockSpec(a.shape, lambda i: (0,) * a.ndim)
    g2 = g_attn.reshape(1, d)
    out_shapes = ([jax.ShapeDtypeStruct((n, 1024), BF16)] * 2 + [jax.ShapeDtypeStruct((n, 512), F32)] * 2
                  + [jax.ShapeDtypeStruct((n, LANES), F32)] * 7)
    out_specs = [row(1024), row(1024), row(512), row(512)] + [row(LANES)] * 7
    return pl.pallas_call(
        _proj_kernel, grid=(n // tm,),
        in_specs=[row(d), full(g2), full(w), full(bias), full(gains), full(hm)],
        out_specs=out_specs, out_shape=out_shapes,
        compiler_params=_cparams(("parallel",)), name="proj",
    )(x2d, g2, w, bias, gains, hm)


def _cumsum_kernel(x_ref, u_ref, o_ref):
    nchunk = x_ref.shape[2] // LANES
    u = u_ref[...]

    def body(i, carry):
        off = pl.multiple_of(i * LANES, LANES)
        hi, mid, lo = _split3(x_ref[0, :, pl.ds(off, LANES)])
        out = _dot(hi, u) + _dot(mid, u) + _dot(lo, u) + carry
        o_ref[0, :, pl.ds(off, LANES)] = out
        return out[:, LANES - 1:LANES]

    lax.fori_loop(0, nchunk, body, jnp.zeros((x_ref.shape[1], 1), F32))


def _cumsum_lanes(xt):
    b, h, s = xt.shape
    i = np.arange(LANES)
    u = jnp.asarray(i[:, None] <= i[None, :], BF16)
    return pl.pallas_call(
        _cumsum_kernel, grid=(b,),
        in_specs=[pl.BlockSpec((1, h, s), lambda i: (i, 0, 0)), pl.BlockSpec((LANES, LANES), lambda i: (0, 0))],
        out_specs=pl.BlockSpec((1, h, s), lambda i: (i, 0, 0)),
        out_shape=jax.ShapeDtypeStruct((b, h, s), F32),
        compiler_params=_cparams(("parallel",)), name="cumsum",
    )(xt, u)


def _tri_pairs(n):
    qi = np.concatenate([np.full(i + 1, i) for i in range(n)]).astype(np.int32)
    ki = np.concatenate([np.arange(i + 1) for i in range(n)]).astype(np.int32)
    return jnp.asarray(qi), jnp.asarray(ki)


def _softmax_step(s, v, m_ref, l_ref, acc_ref, idx):
    m_prev = m_ref[idx]
    m_new = jnp.maximum(m_prev, jnp.max(s, axis=1, keepdims=True))
    alpha = jnp.exp(m_prev - m_new)
    p = jnp.exp(s - m_new)
    l_ref[idx] = alpha * l_ref[idx] + jnp.sum(p, axis=1, keepdims=True)
    acc_ref[idx] = alpha * acc_ref[idx] + _dot(p.astype(BF16), v)
    m_ref[idx] = m_new


def _fox_kernel(qt_ref, kt_ref, q_ref, k_ref, v_ref, ct_ref, o_ref, m_ref, l_ref, acc_ref, *, t):
    hp = pl.program_id(1)
    step = pl.program_id(2)
    qi = qt_ref[step]
    ki = kt_ref[step]

    @pl.when(ki == 0)
    def _():
        m_ref[...] = jnp.full(m_ref.shape, NEG, F32)
        l_ref[...] = jnp.zeros(l_ref.shape, F32)
        acc_ref[...] = jnp.zeros(acc_ref.shape, F32)

    def update(diag):
        k = k_ref[0].astype(BF16)
        v = v_ref[0].astype(BF16)
        if diag:
            causal = (lax.broadcasted_iota(jnp.int32, (t, t), 1) <= lax.broadcasted_iota(jnp.int32, (t, t), 0))
        for h in range(2):
            s = _dot_nt(q_ref[0, :, h * LANES:(h + 1) * LANES], k) - ct_ref[0, pl.ds(2 * hp + h, 1), :]
            if diag:
                s = jnp.where(causal, s, NEG)
            _softmax_step(s, v, m_ref, l_ref, acc_ref, h)

    @pl.when(ki < qi)
    def _():
        update(False)

    @pl.when(ki == qi)
    def _():
        update(True)
        lane = lax.broadcasted_iota(jnp.int32, (t, LANES), 1)
        o = jnp.where(lane < HEAD_DIM, acc_ref[0] / l_ref[0], acc_ref[1] / l_ref[1])
        o_ref[0] = o.astype(BF16)


def _fox_prompt(fq_x, fk, fv, ct, t):
    b, s, _ = fk.shape
    qt, kt = _tri_pairs(s // t)
    grid_spec = pltpu.PrefetchScalarGridSpec(
        num_scalar_prefetch=2, grid=(b, N_FOX_HEADS // 2, qt.shape[0]),
        in_specs=[
            pl.BlockSpec((1, t, 2 * LANES), lambda b, hp, i, qt, kt: (b, qt[i], hp)),
            pl.BlockSpec((1, t, LANES), lambda b, hp, i, qt, kt: (b, kt[i], hp)),
            pl.BlockSpec((1, t, LANES), lambda b, hp, i, qt, kt: (b, kt[i], hp)),
            pl.BlockSpec((1, N_FOX_HEADS, t), lambda b, hp, i, qt, kt: (b, 0, kt[i])),
        ],
        out_specs=pl.BlockSpec((1, t, LANES), lambda b, hp, i, qt, kt: (b, qt[i], hp)),
        scratch_shapes=[pltpu.VMEM((2, t, 1), F32), pltpu.VMEM((2, t, 1), F32), pltpu.VMEM((2, t, LANES), F32)],
    )
    return pl.pallas_call(
        functools.partial(_fox_kernel, t=t), grid_spec=grid_spec,
        out_shape=jax.ShapeDtypeStruct((b, s, 512), BF16),
        compiler_params=_cparams(("parallel", "parallel", "arbitrary")), name="fox_prompt",
    )(qt, kt, fq_x, fk, fv, ct)


def _cmp_weights(w_cmp, pe):
    wl = w_cmp.reshape(CMP_BLOCK, HEAD_DIM, HEAD_DIM)
    eye = jnp.eye(N_NSA_KV, dtype=w_cmp.dtype)
    half = CMP_BLOCK // 2
    w2 = jnp.einsum('lde,gh->lgdhe', wl, eye).reshape(2, half * N_NSA_KV * HEAD_DIM, N_NSA_KV * HEAD_DIM)
    pe2 = jnp.tile(pe.reshape(2, half, 1, HEAD_DIM), (1, 1, N_NSA_KV, 1)).reshape(2, half * N_NSA_KV * HEAD_DIM)
    return w2.astype(BF16), pe2


def _compress(x, w_ref, pe_ref, i0):
    n = x.shape[0]
    a = _dot((x + pe_ref[i0:i0 + 1, :]).astype(BF16), w_ref[i0])
    b = _dot((x + pe_ref[i0 + 1:i0 + 2, :]).astype(BF16), w_ref[i0 + 1])
    return a + pltpu.roll(b, n - 1, axis=0)


def _cmp_kernel(xk_ref, xv_ref, w_ref, pe_ref, g_ref, hm_ref, kc_ref, vc_ref):
    kc = _compress(xk_ref[0], w_ref, pe_ref, 0)
    kc_ref[0] = _head_rms(kc, g_ref[...], hm_ref[...]).astype(BF16)
    vc_ref[0] = _compress(xv_ref[0], w_ref, pe_ref, 2).astype(BF16)


def _cmp_prompt(ck, cv, w4, pe4, g_k_cmp2, hm):
    b, s, _ = ck.shape
    n = s // CMP_STRIDE
    wide = CMP_STRIDE * LANES
    xspec = pl.BlockSpec((1, n, wide), lambda i: (i, 0, 0))
    full = lambda a: pl.BlockSpec(a.shape, lambda i: (0,) * a.ndim)
    ospec = pl.BlockSpec((1, n, LANES), lambda i: (i, 0, 0))
    return pl.pallas_call(
        _cmp_kernel, grid=(b,),
        in_specs=[xspec, xspec, full(w4), full(pe4), full(g_k_cmp2), full(hm)],
        out_specs=[ospec, ospec], out_shape=[jax.ShapeDtypeStruct((b, n, LANES), BF16)] * 2,
        compiler_params=_cparams(("parallel",)), name="cmp_prompt",
    )(ck.reshape(b, n, wide), cv.reshape(b, n, wide), w4, pe4, g_k_cmp2, hm)


def _slope(h):
    return float(2.0 ** (-8.0 * (h + 1) / N_NSA_HEADS))


def _overlap_matrix(nc, nsb):
    c0 = np.arange(nc)[:, None] * CMP_STRIDE
    s0 = np.arange(nsb)[None, :] * SEL_BLOCK
    return jnp.asarray((c0 < s0 + SEL_BLOCK) & (c0 + CMP_BLOCK > s0), BF16)


def _select_blocks(imp, t, nsb):
    jb = lax.broadcasted_iota(jnp.int32, imp.shape, 1)
    cur = t // SEL_BLOCK
    visible = jb * SEL_BLOCK <= t
    work = jnp.where((jb == 0) | (jb == cur) | (jb == cur - 1), FORCE_SCORE, imp)
    work = jnp.where(visible, work, NEG)
    jbf = jb.astype(F32)
    sel = jnp.zeros(imp.shape, F32)
    for _ in range(min(N_SELECT, nsb)):
        mx = jnp.max(work, axis=1, keepdims=True)
        first = jnp.min(jnp.where(work == mx, jbf, float(nsb)), axis=1, keepdims=True)
        pick = jbf == first
        sel = jnp.where(pick, 1.0, sel)
        work = jnp.where(pick, 2.0 * NEG, work)
    return jnp.where(visible, sel, 0.0)


def _cmp_attend(q, kc, vc, d, slope):
    valid = d >= 0.0
    s = jnp.where(valid, _dot_nt(q, kc) - slope * d, NEG)
    m = jnp.max(s, axis=1, keepdims=True)
    e = jnp.where(valid, jnp.exp(s - m), 0.0)
    p = e / jnp.maximum(jnp.sum(e, axis=1, keepdims=True), TINY)
    return p, _dot(p.astype(BF16), vc)


def _nsa_cmp_kernel(q_ref, kc_ref, vc_ref, ov_ref, oc_ref, sel_ref, *, tq, nc_valid):
    qi = pl.program_id(1)
    ncp = kc_ref.shape[1]
    nsb = ov_ref.shape[1]
    t = qi * tq + lax.broadcasted_iota(jnp.int32, (tq, 1), 0)
    n = lax.broadcasted_iota(jnp.int32, (1, ncp), 1)
    cend = jnp.where(n < nc_valid, n * CMP_STRIDE + CMP_BLOCK - 1, jnp.int32(2 ** 30))
    d = (t - cend).astype(F32)
    kc = kc_ref[0]
    vc = vc_ref[0]
    ov = ov_ref[...]
    for g in range(N_NSA_KV):
        psum = jnp.zeros((tq, ncp), F32)
        for j in range(NSA_GROUP):
            h = g * NSA_GROUP + j
            p, o = _cmp_attend(q_ref[0, :, h * LANES:(h + 1) * LANES], kc, vc, d, _slope(h))
            oc_ref[0, :, h * LANES:(h + 1) * LANES] = o
            psum = psum + p
        sel_ref[0, g] = _select_blocks(_dot_x2(psum, ov), t, nsb).astype(BF16)


def _nsa_cmp_prompt(nq_x, kc, vc, tq):
    b, s, _ = nq_x.shape
    ncp = kc.shape[1]
    nc_valid = (s - CMP_BLOCK) // CMP_STRIDE + 1
    nsb = -(-s // SEL_BLOCK)
    ov = _overlap_matrix(ncp, nsb)
    return pl.pallas_call(
        functools.partial(_nsa_cmp_kernel, tq=tq, nc_valid=nc_valid), grid=(b, s // tq),
        in_specs=[pl.BlockSpec((1, tq, 1024), lambda b, i: (b, i, 0)),
                  pl.BlockSpec((1, ncp, LANES), lambda b, i: (b, 0, 0)),
                  pl.BlockSpec((1, ncp, LANES), lambda b, i: (b, 0, 0)),
                  pl.BlockSpec((ncp, nsb), lambda b, i: (0, 0))],
        out_specs=[pl.BlockSpec((1, tq, 1024), lambda b, i: (b, i, 0)),
                   pl.BlockSpec((1, N_NSA_KV, tq, nsb), lambda b, i: (b, 0, i, 0))],
        out_shape=[jax.ShapeDtypeStruct((b, s, 1024), F32), jax.ShapeDtypeStruct((b, N_NSA_KV, s, nsb), BF16)],
        compiler_params=_cparams(("parallel", "parallel")), name="nsa_cmp_prompt",
    )(nq_x, kc, vc, ov)


def _pack_heads(o_a, o_b, g):
    lane = lax.broadcasted_iota(jnp.int32, o_a.shape, 1)
    if g == 0:
        return jnp.where(lane < HEAD_DIM, o_a, pltpu.roll(o_b, HEAD_DIM, axis=1))
    return jnp.where(lane < HEAD_DIM, pltpu.roll(o_a, HEAD_DIM, axis=1), o_b)


def _nsa_mix_kernel(qt_ref, kt_ref, q_ref, sk_ref, sv_ref, sel_ref, wkp_ref, wvp_ref, wkc_ref, wvc_ref,
                    oc_ref, misc_ref, o_ref, m_ref, l_ref, acc_ref, *, t):
    step = pl.program_id(1)
    qi = qt_ref[step]
    ki = kt_ref[step]
    nsb = sel_ref.shape[3]
    per_tile = t // SEL_BLOCK

    @pl.when(ki == 0)
    def _():
        m_ref[...] = jnp.full(m_ref.shape, NEG, F32)
        l_ref[...] = jnp.zeros(l_ref.shape, F32)
        acc_ref[...] = jnp.zeros(acc_ref.shape, F32)

    def update(diag):
        k = sk_ref[0].astype(BF16)
        v = sv_ref[0].astype(BF16)
        col = lax.broadcasted_iota(jnp.int32, (nsb, t), 1)
        blk = lax.broadcasted_iota(jnp.int32, (nsb, t), 0)
        expand = (blk == ki * per_tile + col // SEL_BLOCK).astype(BF16)
        kpos = ((ki - qi) * t + lax.broadcasted_iota(jnp.int32, (1, t), 1)).astype(F32)
        for g in range(N_NSA_KV):
            chosen = _dot(sel_ref[0, g], expand) > 0.5
            if diag:
                chosen = chosen & (lax.broadcasted_iota(jnp.int32, (t, t), 1)
                                   <= lax.broadcasted_iota(jnp.int32, (t, t), 0))
            for j in range(NSA_GROUP):
                h = g * NSA_GROUP + j
                s = _dot_nt(q_ref[0, :, h * LANES:(h + 1) * LANES], k) + _slope(h) * kpos
                _softmax_step(jnp.where(chosen, s, NEG), v, m_ref, l_ref, acc_ref, h)

    @pl.when(ki < qi)
    def _():
        update(False)

    @pl.when(ki == qi)
    def _():
        update(True)
        kw = jnp.concatenate([wkp_ref[0], wkc_ref[0]], axis=0).astype(BF16)
        vw = jnp.concatenate([wvp_ref[0], wvc_ref[0]], axis=0).astype(BF16)
        dw = (t + lax.broadcasted_iota(jnp.int32, (t, 2 * t), 0)
              - lax.broadcasted_iota(jnp.int32, (t, 2 * t), 1))
        kp = (qi - 1) * t + lax.broadcasted_iota(jnp.int32, (1, 2 * t), 1)
        ok = (dw >= 0) & (dw <= WINDOW) & (kp >= 0)
        dwf = dw.astype(F32)
        gates = misc_ref[0]
        outs = []
        for h in range(N_NSA_HEADS):
            q = q_ref[0, :, h * LANES:(h + 1) * LANES]
            s = jnp.where(ok, _dot_nt(q, kw) - _slope(h) * dwf, NEG)
            e = jnp.exp(s - jnp.max(s, axis=1, keepdims=True))
            o_w = _dot(e.astype(BF16), vw) / jnp.sum(e, axis=1, keepdims=True)
            o_s = acc_ref[h] / l_ref[h]
            c0 = N_FOX_HEADS + 3 * h
            outs.append(gates[:, c0:c0 + 1] * oc_ref[0, :, h * LANES:(h + 1) * LANES]
                        + gates[:, c0 + 1:c0 + 2] * o_s + gates[:, c0 + 2:c0 + 3] * o_w)
        for c in range(N_NSA_HEADS // 2):
            o_ref[0, :, c * LANES:(c + 1) * LANES] = _pack_heads(
                outs[2 * c], outs[2 * c + 1], (2 * c) // NSA_GROUP).astype(BF16)


def _nsa_mix_prompt(nq_x, sk, sv, sel, wk, wv, oc_x, misc, t):
    b, s, _ = nq_x.shape
    nsb = sel.shape[3]
    qt, kt = _tri_pairs(s // t)
    qmap = lambda b, i, qt, kt: (b, qt[i], 0)
    kmap = lambda b, i, qt, kt: (b, kt[i], 0)
    pmap = lambda b, i, qt, kt: (b, jnp.maximum(qt[i] - 1, 0), 0)
    grid_spec = pltpu.PrefetchScalarGridSpec(
        num_scalar_prefetch=2, grid=(b, qt.shape[0]),
        in_specs=[
            pl.BlockSpec((1, t, 1024), qmap),
            pl.BlockSpec((1, t, LANES), kmap), pl.BlockSpec((1, t, LANES), kmap),
            pl.BlockSpec((1, N_NSA_KV, t, nsb), lambda b, i, qt, kt: (b, 0, qt[i], 0)),
            pl.BlockSpec((1, t, LANES), pmap), pl.BlockSpec((1, t, LANES), pmap),
            pl.BlockSpec((1, t, LANES), qmap), pl.BlockSpec((1, t, LANES), qmap),
            pl.BlockSpec((1, t, 1024), qmap),
            pl.BlockSpec((1, t, LANES), qmap),
        ],
        out_specs=pl.BlockSpec((1, t, 512), qmap),
        scratch_shapes=[pltpu.VMEM((N_NSA_HEADS, t, 1), F32), pltpu.VMEM((N_NSA_HEADS, t, 1), F32),
                        pltpu.VMEM((N_NSA_HEADS, t, LANES), F32)],
    )
    return pl.pallas_call(
        functools.partial(_nsa_mix_kernel, t=t), grid_spec=grid_spec,
        out_shape=jax.ShapeDtypeStruct((b, s, 512), BF16),
        compiler_params=_cparams(("parallel", "arbitrary")), name="nsa_mix_prompt",
    )(qt, kt, nq_x, sk, sv, sel, wk, wv, wk, wv, oc_x, misc)


_ROUTE_E, _ROUTE_GATE, _ROUTE_RANK = 0, TOP_K, 2 * TOP_K


def _router_weights(w_router, b_router):
    d = w_router.shape[0]
    wp = jnp.concatenate([w_router, jnp.zeros((d, LANES - N_EXPERTS), F32)], axis=1)
    hi = wp.astype(BF16)
    lo = (wp - hi.astype(F32)).astype(BF16)
    bias = jnp.concatenate([b_router.astype(F32), jnp.full((LANES - N_EXPERTS,), NEG, F32)]).reshape(1, LANES)
    return hi, lo, bias


def _tail_kernel(x_ref, mf_ref, mn_ref, wo_ref, g_ref, wrh_ref, wrl_ref, br_ref, tri_ref,
                 x1_ref, h_ref, route_ref, cnt_ref):
    @pl.when(pl.program_id(0) == 0)
    def _():
        cnt_ref[...] = jnp.zeros(cnt_ref.shape, F32)

    half = mf_ref.shape[1]
    x1 = x_ref[...] + _dot(mf_ref[...], wo_ref[0:half, :]) + _dot(mn_ref[...], wo_ref[half:2 * half, :])
    x1_ref[...] = x1
    h = (x1 * lax.rsqrt(jnp.mean(x1 * x1, axis=-1, keepdims=True) + RMS_EPS)) * g_ref[...]
    h_ref[...] = h
    h_hi, h_lo = _split2(h)
    logits = _dot(h_hi, wrh_ref[...]) + _dot(h_lo, wrh_ref[...]) + _dot(h_hi, wrl_ref[...]) + br_ref[...]
    lane = lax.broadcasted_iota(jnp.int32, logits.shape, 1)
    lanef = lane.astype(F32)
    work = logits
    picks, vals, ids = [], [], []
    for _ in range(TOP_K):
        mx = jnp.max(work, axis=1, keepdims=True)
        first = jnp.min(jnp.where(work == mx, lanef, float(LANES)), axis=1, keepdims=True)
        pick = lanef == first
        work = jnp.where(pick, 2.0 * NEG, work)
        picks.append(pick); vals.append(mx); ids.append(first)
    ex = [jnp.exp(v - vals[0]) for v in vals]
    den = ex[0] + ex[1] + ex[2] + ex[3]
    onehot = jnp.zeros(logits.shape, F32)
    for p in picks:
        onehot = jnp.where(p, 1.0, onehot)
    carry = cnt_ref[0:1, :]
    ranks = _dot(tri_ref[...], onehot.astype(BF16)) + carry
    route = jnp.zeros(logits.shape, F32)
    for k in range(TOP_K):
        rank_k = jnp.sum(jnp.where(picks[k], ranks, 0.0), axis=1, keepdims=True)
        route = jnp.where(lane == _ROUTE_E + k, ids[k], route)
        route = jnp.where(lane == _ROUTE_GATE + k, ex[k] / den, route)
        route = jnp.where(lane == _ROUTE_RANK + k, rank_k, route)
    route_ref[...] = route
    cnt_ref[...] = jnp.broadcast_to(carry + jnp.sum(onehot, axis=0, keepdims=True), cnt_ref.shape)


def _tail(x2d, mf, mn, wo, g_ffn, wrh, wrl, br, tm):
    n, d = x2d.shape
    i = np.arange(tm)
    tri = jnp.asarray(i[:, None] > i[None, :], BF16)
    row = lambda c: pl.BlockSpec((tm, c), lambda i: (i, 0))
    full = lambda a: pl.BlockSpec(a.shape, lambda i: (0,) * a.ndim)
    g2 = g_ffn.reshape(1, d)
    return pl.pallas_call(
        _tail_kernel, grid=(n // tm,),
        in_specs=[row(d), row(mf.shape[1]), row(mn.shape[1]), full(wo), full(g2), full(wrh), full(wrl), full(br),
                  full(tri)],
        out_specs=[row(d), row(d), row(LANES), pl.BlockSpec((8, LANES), lambda i: (0, 0))],
        out_shape=[jax.ShapeDtypeStruct((n, d), F32), jax.ShapeDtypeStruct((n, d), F32),
                   jax.ShapeDtypeStruct((n, LANES), F32), jax.ShapeDtypeStruct((8, LANES), F32)],
        compiler_params=_cparams(("arbitrary",)), name="tail",
    )(x2d, mf, mn, wo, g2, wrh, wrl, br, tri)


def _row_copy(src_hbm, src_row, dst, dst_row, sem):
    return pltpu.make_async_copy(src_hbm.at[pl.ds(src_row, 1)], dst.at[pl.ds(dst_row, 1)], sem)


def _dispatch_kernel(dest_ref, h_hbm, xs_in_hbm, xs_hbm, sem, *, tm):
    del xs_in_hbm
    base = pl.program_id(0) * tm

    def issue(r, c):
        for k in range(TOP_K):
            _row_copy(h_hbm, base + r, xs_hbm, dest_ref[0, 0, TOP_K * r + k], sem).start()
        return c

    def drain(r, c):
        for k in range(TOP_K):
            _row_copy(h_hbm, base + r, xs_hbm, dest_ref[0, 0, TOP_K * r + k], sem).wait()
        return c

    lax.fori_loop(0, tm, issue, 0)
    lax.fori_loop(0, tm, drain, 0)


def _dispatch(h, dest, xs, tm):
    n, d = h.shape
    dest3 = dest.reshape(n // tm, 1, tm * TOP_K)
    return pl.pallas_call(
        functools.partial(_dispatch_kernel, tm=tm), grid=(n // tm,),
        in_specs=[pl.BlockSpec((1, 1, tm * TOP_K), lambda i: (i, 0, 0), memory_space=pltpu.SMEM),
                  pl.BlockSpec(memory_space=pl.ANY), pl.BlockSpec(memory_space=pl.ANY)],
        out_specs=pl.BlockSpec(memory_space=pl.ANY),
        out_shape=jax.ShapeDtypeStruct(xs.shape, xs.dtype),
        scratch_shapes=[pltpu.SemaphoreType.DMA(())],
        input_output_aliases={2: 0},
        compiler_params=_cparams(("arbitrary",)), name="moe_dispatch",
    )(dest3, h, xs)


def _expert_kernel(be_ref, nused_ref, x_ref, wg_ref, wu_ref, bg_ref, bu_ref, wd_ref, bd_ref, y_ref):
    @pl.when(pl.program_id(0) < nused_ref[0])
    def _():
        x = x_ref[...].astype(BF16)
        g = jnp.minimum(_dot(x, wg_ref[0]) + bg_ref[0], SWIGLU_LIMIT)
        u = jnp.clip(_dot(x, wu_ref[0]) + bu_ref[0], -SWIGLU_LIMIT, SWIGLU_LIMIT)
        act = (u + 1.0) * g * jax.nn.sigmoid(SWIGLU_ALPHA * g)
        y_ref[...] = _dot(act.astype(BF16), wd_ref[0]) + bd_ref[0]

    @pl.when(pl.program_id(0) >= nused_ref[0])
    def _():
        y_ref[...] = jnp.zeros(y_ref.shape, F32)


def _experts(xs, blk_expert, nused, wg, wu, bg, bu, wd, bd, mb):
    n_slots, d = xs.shape
    de = wg.shape[2]
    rows = lambda i, be, nu: (jnp.minimum(i, nu[0] - 1), 0)
    wmap = lambda i, be, nu: (be[i], 0, 0)
    grid_spec = pltpu.PrefetchScalarGridSpec(
        num_scalar_prefetch=2, grid=(n_slots // mb,),
        in_specs=[pl.BlockSpec((mb, d), rows),
                  pl.BlockSpec((1, d, de), wmap), pl.BlockSpec((1, d, de), wmap),
                  pl.BlockSpec((1, 1, de), wmap), pl.BlockSpec((1, 1, de), wmap),
                  pl.BlockSpec((1, de, d), wmap), pl.BlockSpec((1, 1, d), wmap)],
        out_specs=pl.BlockSpec((mb, d), lambda i, be, nu: (i, 0)),
    )
    return pl.pallas_call(
        _expert_kernel, grid_spec=grid_spec, out_shape=jax.ShapeDtypeStruct((n_slots, d), F32),
        compiler_params=_cparams(("arbitrary",)), name="moe_experts",
    )(blk_expert, nused, xs, wg, wu, bg, bu, wd, bd)


def _combine_kernel(dest_ref, route_ref, x1_ref, ys_hbm, o_ref, buf_ref, sem, *, tm):
    def issue(r, c):
        for k in range(TOP_K):
            _row_copy(ys_hbm, dest_ref[0, 0, TOP_K * r + k], buf_ref.at[k], r, sem).start()
        return c

    def drain(r, c):
        for k in range(TOP_K):
            _row_copy(ys_hbm, dest_ref[0, 0, TOP_K * r + k], buf_ref.at[k], r, sem).wait()
        return c

    lax.fori_loop(0, tm, issue, 0)
    lax.fori_loop(0, tm, drain, 0)
    route = route_ref[...]
    y = route[:, _ROUTE_GATE:_ROUTE_GATE + 1] * buf_ref[0]
    for k in range(1, TOP_K):
        y = y + route[:, _ROUTE_GATE + k:_ROUTE_GATE + k + 1] * buf_ref[k]
    o_ref[...] = x1_ref[...] + y


def _combine(x1, route, dest, ys, tm):
    n, d = x1.shape
    dest3 = dest.reshape(n // tm, 1, tm * TOP_K)
    row = lambda c: pl.BlockSpec((tm, c), lambda i: (i, 0))
    return pl.pallas_call(
        functools.partial(_combine_kernel, tm=tm), grid=(n // tm,),
        in_specs=[pl.BlockSpec((1, 1, tm * TOP_K), lambda i: (i, 0, 0), memory_space=pltpu.SMEM),
                  row(LANES), row(d), pl.BlockSpec(memory_space=pl.ANY)],
        out_specs=row(d), out_shape=jax.ShapeDtypeStruct((n, d), F32),
        scratch_shapes=[pltpu.VMEM((TOP_K, tm, d), F32), pltpu.SemaphoreType.DMA(())],
        compiler_params=_cparams(("arbitrary",)), name="moe_combine",
    )(dest3, route, x1, ys)


def _moe_plan(route_p, cnt_p, route_s, cnt_s, mb):
    n_tok = route_p.shape[0] + route_s.shape[0]
    cp = cnt_p[0, :N_EXPERTS].astype(jnp.int32)
    cs = cnt_s[0, :N_EXPERTS].astype(jnp.int32)
    counts = cp + cs
    padded = (counts + mb - 1) // mb * mb
    pend = jnp.cumsum(padded)
    pstart = pend - padded
    nblk = -(-n_tok * TOP_K // mb) + N_EXPERTS
    blk_expert = jnp.clip(jnp.searchsorted(pend, jnp.arange(nblk, dtype=jnp.int32) * mb, side='right'),
                          0, N_EXPERTS - 1).astype(jnp.int32)
    nused = (pend[-1:] // mb).astype(jnp.int32)

    def dest(route, first):
        e = route[:, _ROUTE_E:_ROUTE_E + TOP_K].astype(jnp.int32)
        rank = route[:, _ROUTE_RANK:_ROUTE_RANK + TOP_K].astype(jnp.int32)
        onehot = e[..., None] == jnp.arange(N_EXPERTS, dtype=jnp.int32)
        return jnp.sum(jnp.where(onehot, first, 0), axis=-1) + rank

    return dest(route_p, pstart), dest(route_s, pstart + cp), blk_expert, nused, nblk


def _page_spec(shape, n_pages, pp, j):
    return pl.BlockSpec((1,) + shape, lambda b, c, pt: (pt[b * n_pages + c * pp + j],) + (0,) * len(shape))


def _slope_col():
    h = lax.broadcasted_iota(jnp.int32, (N_NSA_HEADS, 1), 0).astype(F32)
    return jnp.exp2(-8.0 * (h + 1.0) / N_NSA_HEADS)


def _rounded(ref):
    return ref[0].astype(BF16).astype(F32)


def _append_key(q, s_bias, k_new, v_new, m_ref, l_ref, acc_ref):
    s = jnp.sum(q.astype(F32) * k_new, axis=1, keepdims=True) + s_bias
    m_prev = m_ref[0]
    m_new = jnp.maximum(m_prev, s)
    alpha = jnp.exp(m_prev - m_new)
    p = jnp.exp(s - m_new)
    l_ref[0] = alpha * l_ref[0] + p
    acc_ref[0] = alpha * acc_ref[0] + p * v_new
    m_ref[0] = m_new


def _init_softmax(m_ref, l_ref, acc_ref):
    m_ref[...] = jnp.full(m_ref.shape, NEG, F32)
    l_ref[...] = jnp.zeros(l_ref.shape, F32)
    acc_ref[...] = jnp.zeros(acc_ref.shape, F32)


def _fox_sample_kernel(pt_ref, q_ref, *refs, pp):
    del pt_ref
    k_refs, v_refs = refs[:pp], refs[pp:2 * pp]
    ct_ref, kn_ref, vn_ref, cn_ref, o_ref, m_ref, l_ref, acc_ref = refs[2 * pp:]
    c = pl.program_id(1)

    @pl.when(c == 0)
    def _():
        _init_softmax(m_ref, l_ref, acc_ref)

    q = q_ref[0]
    for j in range(pp):
        s = _dot_nt(q, k_refs[j][0].astype(BF16)) - ct_ref[0, :, j * PAGE_SIZE:(j + 1) * PAGE_SIZE]
        _softmax_step(s, v_refs[j][0].astype(BF16), m_ref, l_ref, acc_ref, 0)

    @pl.when(c == pl.num_programs(1) - 1)
    def _():
        _append_key(q, -cn_ref[0][:, 0:1], _rounded(kn_ref), _rounded(vn_ref), m_ref, l_ref, acc_ref)
        o_ref[0] = acc_ref[0] / l_ref[0]


def _fox_sample(pt, qt, cache_k, cache_v, ct, k_new, v_new, c_new, pp):
    b, n_pages = pt.shape
    w = cache_k.shape[2]
    bmap = lambda b, c, pt: (b, 0, 0)
    grid_spec = pltpu.PrefetchScalarGridSpec(
        num_scalar_prefetch=1, grid=(b, n_pages // pp),
        in_specs=([pl.BlockSpec((1, N_FOX_HEADS, w), bmap)]
                  + [_page_spec((PAGE_SIZE, w), n_pages, pp, j) for j in range(pp)] * 2
                  + [pl.BlockSpec((1, N_FOX_HEADS, pp * PAGE_SIZE), lambda b, c, pt: (b, 0, c)),
                     pl.BlockSpec((1, 1, w), bmap), pl.BlockSpec((1, 1, w), bmap),
                     pl.BlockSpec((1, N_FOX_HEADS, LANES), bmap)]),
        out_specs=pl.BlockSpec((1, N_FOX_HEADS, w), bmap),
        scratch_shapes=[pltpu.VMEM((1, N_FOX_HEADS, 1), F32), pltpu.VMEM((1, N_FOX_HEADS, 1), F32),
                        pltpu.VMEM((1, N_FOX_HEADS, w), F32)],
    )
    return pl.pallas_call(
        functools.partial(_fox_sample_kernel, pp=pp), grid_spec=grid_spec,
        out_shape=jax.ShapeDtypeStruct((b, N_FOX_HEADS, w), F32),
        compiler_params=_cparams(("parallel", "arbitrary")), name="fox_sample",
    )(pt.reshape(-1), qt, *([cache_k] * pp), *([cache_v] * pp), ct, k_new, v_new, c_new)


def _cmp_sample_kernel(pt_ref, q_ref, *refs, pp, past):
    del pt_ref
    xk_refs, xv_refs = refs[:pp], refs[pp:2 * pp]
    w_ref, pe_ref, g_ref, hm_ref, ov_ref, oc_ref, sel_ref, a_ref = refs[2 * pp:]
    c = pl.program_id(1)
    rpp = PAGE_SIZE // CMP_STRIDE
    xk = jnp.concatenate([r[0] for r in xk_refs], axis=0)
    xv = jnp.concatenate([r[0] for r in xv_refs], axis=0)
    rows = pl.ds(pl.multiple_of(c * (rpp * pp), rpp * pp), rpp * pp)
    for i, x in ((0, xk), (1, xk), (2, xv), (3, xv)):
        a_ref[i, rows, :] = _dot((x + pe_ref[i:i + 1, :]).astype(BF16), w_ref[i])

    @pl.when(c == pl.num_programs(1) - 1)
    def _():
        n = a_ref.shape[1]
        nsbp = ov_ref.shape[1]
        kc = _head_rms(a_ref[0] + pltpu.roll(a_ref[1], n - 1, axis=0), g_ref[...], hm_ref[...]).astype(BF16)
        vc = (a_ref[2] + pltpu.roll(a_ref[3], n - 1, axis=0)).astype(BF16)
        nc_valid = (past + 1 - CMP_BLOCK) // CMP_STRIDE + 1
        idx = lax.broadcasted_iota(jnp.int32, (N_NSA_HEADS, n), 1)
        cend = jnp.where(idx < nc_valid, idx * CMP_STRIDE + CMP_BLOCK - 1, jnp.int32(2 ** 30))
        p, o = _cmp_attend(q_ref[0], kc, vc, (past - cend).astype(F32), _slope_col())
        oc_ref[0] = o
        psum = jnp.concatenate([jnp.sum(p[0:NSA_GROUP], axis=0, keepdims=True),
                                jnp.sum(p[NSA_GROUP:2 * NSA_GROUP], axis=0, keepdims=True),
                                jnp.zeros((N_NSA_HEADS - N_NSA_KV, n), F32)], axis=0)
        t = jnp.full((N_NSA_HEADS, 1), past, jnp.int32)
        sel = _select_blocks(_dot_x2(psum, ov_ref[...]), t, nsbp)
        sel_ref[0] = jnp.concatenate([jnp.broadcast_to(sel[0:1], (NSA_GROUP, nsbp)),
                                      jnp.broadcast_to(sel[1:2], (NSA_GROUP, nsbp))], axis=0).astype(BF16)


def _cmp_sample(pt, qg, cache_ck, cache_cv, w4, pe4, g_k_cmp2, hm, pp):
    b, n_pages = pt.shape
    past = n_pages * PAGE_SIZE
    wide = cache_ck.shape[2]
    rpp = PAGE_SIZE // CMP_STRIDE
    n = n_pages * rpp
    nsb = -(-(past + 1) // SEL_BLOCK)
    nsbp = -(-nsb // LANES) * LANES
    ov = _overlap_matrix(n, nsbp)
    bmap = lambda b, c, pt: (b, 0, 0)
    full = lambda a: pl.BlockSpec(a.shape, lambda b, c, pt: (0,) * a.ndim)
    grid_spec = pltpu.PrefetchScalarGridSpec(
        num_scalar_prefetch=1, grid=(b, n_pages // pp),
        in_specs=([pl.BlockSpec((1, N_NSA_HEADS, LANES), bmap)]
                  + [_page_spec((rpp, wide), n_pages, pp, j) for j in range(pp)] * 2
                  + [full(w4), full(pe4), full(g_k_cmp2), full(hm), full(ov)]),
        out_specs=[pl.BlockSpec((1, N_NSA_HEADS, LANES), bmap), pl.BlockSpec((1, N_NSA_HEADS, nsbp), bmap)],
        scratch_shapes=[pltpu.VMEM((4, n, LANES), F32)],
    )
    return pl.pallas_call(
        functools.partial(_cmp_sample_kernel, pp=pp, past=past), grid_spec=grid_spec,
        out_shape=[jax.ShapeDtypeStruct((b, N_NSA_HEADS, LANES), F32),
                   jax.ShapeDtypeStruct((b, N_NSA_HEADS, nsbp), BF16)],
        compiler_params=_cparams(("parallel", "arbitrary")), name="cmp_sample",
    )(pt.reshape(-1), qg, *([cache_ck] * pp), *([cache_cv] * pp), w4, pe4, g_k_cmp2, hm, ov)


def _mix_sample_kernel(pt_ref, q_ref, sel_ref, *refs, pp, past):
    del pt_ref
    k_refs, v_refs = refs[:pp], refs[pp:2 * pp]
    (skn_ref, svn_ref, stk_ref, stv_ref, wkn_ref, wvn_ref, oc_ref, gate_ref,
     o_ref, m_ref, l_ref, acc_ref) = refs[2 * pp:]
    c = pl.program_id(1)
    tc = pp * PAGE_SIZE
    nsbp = sel_ref.shape[2]

    @pl.when(c == 0)
    def _():
        _init_softmax(m_ref, l_ref, acc_ref)

    q = q_ref[0]
    slope = _slope_col()
    k = jnp.concatenate([r[0] for r in k_refs], axis=0).astype(BF16)
    v = jnp.concatenate([r[0] for r in v_refs], axis=0).astype(BF16)
    col = lax.broadcasted_iota(jnp.int32, (nsbp, tc), 1)
    blk = lax.broadcasted_iota(jnp.int32, (nsbp, tc), 0)
    expand = (blk == c * (tc // SEL_BLOCK) + col // SEL_BLOCK).astype(BF16)
    chosen = _dot(sel_ref[0], expand) > 0.5
    kpos = (c * tc - past + lax.broadcasted_iota(jnp.int32, (1, tc), 1)).astype(F32)
    s = _dot_nt(q, k) + slope * kpos
    _softmax_step(jnp.where(chosen, s, NEG), v, m_ref, l_ref, acc_ref, 0)

    @pl.when(c == pl.num_programs(1) - 1)
    def _():
        _append_key(q, 0.0, _rounded(skn_ref), _rounded(svn_ref), m_ref, l_ref, acc_ref)
        o_s = acc_ref[0] / l_ref[0]
        nbuf = stk_ref.shape[1]
        dwin = (nbuf - lax.broadcasted_iota(jnp.int32, (1, nbuf), 1)).astype(F32)
        s_w = _dot_nt(q, stk_ref[0].astype(BF16)) - slope * dwin
        s_n = jnp.sum(q.astype(F32) * _rounded(wkn_ref), axis=1, keepdims=True)
        mw = jnp.maximum(jnp.max(s_w, axis=1, keepdims=True), s_n)
        e = jnp.exp(s_w - mw)
        e_n = jnp.exp(s_n - mw)
        o_w = ((_dot(e.astype(BF16), stv_ref[0].astype(BF16)) + e_n * _rounded(wvn_ref))
               / (jnp.sum(e, axis=1, keepdims=True) + e_n))
        g = gate_ref[0]
        o_ref[0] = g[:, 0:1] * oc_ref[0] + g[:, 1:2] * o_s + g[:, 2:3] * o_w


def _mix_sample(pt, qg, selh, cache_sk, cache_sv, sk_new, sv_new, st_k, st_v, wk_new, wv_new, oc, gates, pp):
    b, n_pages = pt.shape
    past = n_pages * PAGE_SIZE
    nsbp = selh.shape[2]
    nbuf = st_k.shape[1]
    bmap = lambda b, c, pt: (b, 0, 0)
    one = pl.BlockSpec((1, 1, LANES), bmap)
    grid_spec = pltpu.PrefetchScalarGridSpec(
        num_scalar_prefetch=1, grid=(b, n_pages // pp),
        in_specs=([pl.BlockSpec((1, N_NSA_HEADS, LANES), bmap), pl.BlockSpec((1, N_NSA_HEADS, nsbp), bmap)]
                  + [_page_spec((PAGE_SIZE, LANES), n_pages, pp, j) for j in range(pp)] * 2
                  + [one, one, pl.BlockSpec((1, nbuf, LANES), bmap), pl.BlockSpec((1, nbuf, LANES), bmap), one, one,
                     pl.BlockSpec((1, N_NSA_HEADS, LANES), bmap), pl.BlockSpec((1, N_NSA_HEADS, 3), bmap)]),
        out_specs=pl.BlockSpec((1, N_NSA_HEADS, LANES), bmap),
        scratch_shapes=[pltpu.VMEM((1, N_NSA_HEADS, 1), F32), pltpu.VMEM((1, N_NSA_HEADS, 1), F32),
                        pltpu.VMEM((1, N_NSA_HEADS, LANES), F32)],
    )
    return pl.pallas_call(
        functools.partial(_mix_sample_kernel, pp=pp, past=past), grid_spec=grid_spec,
        out_shape=jax.ShapeDtypeStruct((b, N_NSA_HEADS, LANES), F32),
        compiler_params=_cparams(("parallel", "arbitrary")), name="mix_sample",
    )(pt.reshape(-1), qg, selh, *([cache_sk] * pp), *([cache_sv] * pp), sk_new, sv_new, st_k, st_v,
      wk_new, wv_new, oc, gates)


PROMPT_ROWS = 256
ATTN_TILE = 512
CMP_Q_TILE = 256
MOE_ROWS = 256
PAGES_PER_STEP = 8


def _layer(xp, xs, c_fox_k, c_fox_v, c_fox_logf, c_cmp_k, c_cmp_v, c_sel_k, c_sel_v, st_win_k, st_win_v, page_table,
           g_attn, w_in, b_fox_f, b_nsa_gate, g_q_fox, g_k_fox, g_q_nsa, g_k_sel, g_k_win, g_k_cmp,
           pe_cmp_k, pe_cmp_v, w_cmp_k, w_cmp_v, w_out, g_ffn, w_router, b_router, w_gate_up, b_gate_up,
           w_down, b_down):
    bp, sp, d = xp.shape
    bs = xs.shape[0]
    n_pool = c_fox_k.shape[0]
    n_pages = page_table.shape[1]
    past = n_pages * PAGE_SIZE

    w, bias = _proj_weights(w_in, b_fox_f, b_nsa_gate)
    tile2 = lambda g: jnp.tile(g, 2)
    gains = jnp.stack([tile2(g_q_fox), tile2(g_q_nsa), tile2(g_k_fox), tile2(g_k_sel), tile2(g_k_win)])
    hm = _head_mean_matrix()
    wk2, pek = _cmp_weights(w_cmp_k, pe_cmp_k)
    wv2, pev = _cmp_weights(w_cmp_v, pe_cmp_v)
    w4 = jnp.concatenate([wk2, wv2])
    pe4 = jnp.concatenate([pek, pev])
    g_cmp2 = tile2(g_k_cmp).reshape(1, LANES)
    wo = w_out.astype(BF16)
    wrh, wrl, br = _router_weights(w_router, b_router)
    wg = w_gate_up[:, :, 0::2].astype(BF16)
    wu = w_gate_up[:, :, 1::2].astype(BF16)
    bg = b_gate_up[:, None, 0::2]
    bu = b_gate_up[:, None, 1::2]
    wd = w_down.astype(BF16)
    bd = b_down[:, None, :]

    pr = _project(xp.reshape(bp * sp, d), g_attn, w, bias, gains, hm, PROMPT_ROWS)
    fq, nq, fk, fv, ck, cv, sk, sv, wk, wv, misc = [a.reshape(bp, sp, -1) for a in pr]
    logf = misc[:, :, :N_FOX_HEADS]
    ct = _cumsum_lanes(jnp.transpose(logf, (0, 2, 1)))
    mix_fox = _fox_prompt(fq, fk, fv, ct, ATTN_TILE)
    kc, vc = _cmp_prompt(ck, cv, w4, pe4, g_cmp2, hm)
    oc, sel = _nsa_cmp_prompt(nq, kc, vc, CMP_Q_TILE)
    mix_nsa = _nsa_mix_prompt(nq, sk, sv, sel, wk, wv, oc, misc, ATTN_TILE)
    x1p, hp, route_p, cnt_p = _tail(xp.reshape(bp * sp, d), mix_fox.reshape(bp * sp, -1),
                                    mix_nsa.reshape(bp * sp, -1), wo, g_ffn, wrh, wrl, br, PROMPT_ROWS)

    sr = _project(xs.reshape(bs, d), g_attn, w, bias, gains, hm, bs)
    fq_s, nq_s, fk_s, fv_s, ck_s, cv_s, sk_s, sv_s, wk_s, wv_s, misc_s = sr
    logf_s = misc_s[:, :N_FOX_HEADS]
    gates_s = misc_s[:, N_FOX_HEADS:N_FOX_HEADS + 3 * N_NSA_HEADS].reshape(bs, N_NSA_HEADS, 3)
    q_heads = fq_s.reshape(bs, N_FOX_HEADS, 2, HEAD_DIM)
    q_heads = jnp.stack([q_heads[:, h, h % 2] for h in range(N_FOX_HEADS)], axis=1)
    qt = jnp.einsum('bhd,hg->bhgd', q_heads, jnp.eye(N_FOX_HEADS, dtype=BF16)).reshape(bs, N_FOX_HEADS, -1)
    qg = nq_s.reshape(bs, N_NSA_HEADS, LANES)
    logf_past = c_fox_logf[page_table].reshape(bs, past, N_FOX_HEADS).astype(F32)
    ct_s = _cumsum_lanes(jnp.transpose(logf_past, (0, 2, 1)))
    c_new = jnp.broadcast_to((ct_s[:, :, -1] + logf_s)[:, :, None], (bs, N_FOX_HEADS, LANES))
    row3 = lambda a: a.reshape(bs, 1, -1)
    o_fox_s = _fox_sample(page_table, qt, c_fox_k.reshape(n_pool, PAGE_SIZE, -1), c_fox_v.reshape(n_pool, PAGE_SIZE, -1),
                          ct_s, row3(fk_s), row3(fv_s), c_new, PAGES_PER_STEP)
    wide = CMP_STRIDE * LANES
    oc_s, selh = _cmp_sample(page_table, qg, c_cmp_k.reshape(n_pool, -1, wide), c_cmp_v.reshape(n_pool, -1, wide),
                             w4, pe4, g_cmp2, hm, PAGES_PER_STEP)
    nbuf = st_win_k.shape[1]
    o_nsa_s = _mix_sample(page_table, qg, selh, c_sel_k.reshape(n_pool, PAGE_SIZE, -1),
                          c_sel_v.reshape(n_pool, PAGE_SIZE, -1), row3(sk_s), row3(sv_s),
                          st_win_k.reshape(bs, nbuf, -1), st_win_v.reshape(bs, nbuf, -1), row3(wk_s), row3(wv_s),
                          oc_s, gates_s, PAGES_PER_STEP)
    o_fox_s = o_fox_s.reshape(bs, N_FOX_HEADS, N_FOX_HEADS, HEAD_DIM)
    mf_s = jnp.stack([o_fox_s[:, h, h] for h in range(N_FOX_HEADS)], axis=1).reshape(bs, -1).astype(BF16)
    o_nsa_s = o_nsa_s.reshape(bs, N_NSA_HEADS, N_NSA_KV, HEAD_DIM)
    mn_s = jnp.stack([o_nsa_s[:, h, h // NSA_GROUP] for h in range(N_NSA_HEADS)], axis=1).reshape(bs, -1).astype(BF16)
    x1s, hs, route_s, cnt_s = _tail(xs.reshape(bs, d), mf_s, mn_s, wo, g_ffn, wrh, wrl, br, bs)

    dest_p, dest_s, blk_expert, nused, nblk = _moe_plan(route_p, cnt_p, route_s, cnt_s, MOE_ROWS)
    slots = jnp.zeros((nblk * MOE_ROWS, d), F32)
    slots = _dispatch(hp, dest_p, slots, PROMPT_ROWS)
    slots = _dispatch(hs, dest_s, slots, bs)
    ys = _experts(slots, blk_expert, nused, wg, wu, bg, bu, wd, bd, MOE_ROWS)
    yp = _combine(x1p, route_p, dest_p, ys, PROMPT_ROWS).reshape(bp, sp, d)
    y_s = _combine(x1s, route_s, dest_s, ys, bs).reshape(bs, 1, d)

    heads = lambda a, b_, s_: a.reshape(b_, s_, -1, HEAD_DIM)
    wp = min(WINDOW, sp)
    rows_p = (heads(fk, bp, sp), heads(fv, bp, sp), logf, heads(ck, bp, sp), heads(cv, bp, sp), heads(sk, bp, sp),
              heads(sv, bp, sp), heads(wk, bp, sp)[:, -wp:], heads(wv, bp, sp)[:, -wp:])
    win_k_s = jnp.concatenate([st_win_k, heads(wk_s, bs, 1)], axis=1)[:, -nbuf:]
    win_v_s = jnp.concatenate([st_win_v, heads(wv_s, bs, 1)], axis=1)[:, -nbuf:]
    rows_s = (heads(fk_s, bs, 1), heads(fv_s, bs, 1), logf_s.reshape(bs, 1, -1), heads(ck_s, bs, 1),
              heads(cv_s, bs, 1), heads(sk_s, bs, 1), heads(sv_s, bs, 1), win_k_s, win_v_s)
    return yp, y_s, rows_p, rows_s


def kernel(x_prompt, x_sample, cache_fox_k, cache_fox_v, cache_fox_logf, cache_cmp_k, cache_cmp_v, cache_sel_k,
           cache_sel_v, state_win_k, state_win_v, page_table, g_attn, w_in, b_fox_f, b_nsa_gate, g_q_fox, g_k_fox,
           g_q_nsa, g_k_sel, g_k_win, g_k_cmp, pe_cmp_k, pe_cmp_v, w_cmp_k, w_cmp_v, w_out, g_ffn, w_router,
           b_router, w_gate_up, b_gate_up, w_down, b_down):
    depth = w_in.shape[0]
    assert depth == 1, "the two groups are threaded through a single layer"
    l = 0
    yp, y_s, rows_p, rows_s = _layer(
        x_prompt, x_sample, cache_fox_k[l], cache_fox_v[l], cache_fox_logf[l], cache_cmp_k[l], cache_cmp_v[l],
        cache_sel_k[l], cache_sel_v[l], state_win_k[l], state_win_v[l], page_table,
        g_attn[l], w_in[l], b_fox_f[l], b_nsa_gate[l], g_q_fox[l], g_k_fox[l], g_q_nsa[l], g_k_sel[l], g_k_win[l],
        g_k_cmp[l], pe_cmp_k[l], pe_cmp_v[l], w_cmp_k[l], w_cmp_v[l], w_out[l], g_ffn[l], w_router[l], b_router[l],
        w_gate_up[l], b_gate_up[l], w_down[l], b_down[l])
    return (yp, y_s) + tuple(r[None] for r in rows_p) + tuple(r[None] for r in rows_s)
```

```python
import functools

import numpy as np
import jax
import jax.numpy as jnp
from jax import lax
from jax.experimental import pallas as pl
from jax.experimental.pallas import tpu as pltpu

F32 = jnp.float32
BF16 = jnp.bfloat16

HEAD_DIM = 64
LANES = 128
N_FOX_HEADS = 8
N_NSA_HEADS = 8
N_NSA_KV = 2
NSA_GROUP = N_NSA_HEADS // N_NSA_KV
CMP_BLOCK = 32
CMP_STRIDE = 16
SEL_BLOCK = 64
N_SELECT = 16
WINDOW = 512
FORCE_SCORE = 1.0e4
N_EXPERTS = 32
TOP_K = 4
SWIGLU_LIMIT = 7.0
SWIGLU_ALPHA = 1.702
RMS_EPS = 1e-6
PAGE_SIZE = 128
NEG = -1.0e30
LOG2E = float(np.log2(np.e))
LN2 = float(np.log(2.0))
N_BIAS_LANES = 3
MASK_SCORE = -float(2 ** 17)
TINY = float(np.finfo(np.float32).tiny)
VMEM_LIMIT = 56 * 1024 * 1024


def _cparams(sem):
    return pltpu.CompilerParams(dimension_semantics=sem, vmem_limit_bytes=VMEM_LIMIT)


def _dot(a, b):
    return jnp.dot(a, b, preferred_element_type=F32)


def _dot_nt(a, b):
    return lax.dot_general(a, b, (((1,), (1,)), ((), ())), preferred_element_type=F32)


def _dot_tn(a, b):
    return lax.dot_general(a, b, (((0,), (0,)), ((), ())), preferred_element_type=F32)


def _split2(x):
    hi = x.astype(BF16)
    lo = (x - hi.astype(F32)).astype(BF16)
    return hi, lo


def _split3(x):
    hi = x.astype(BF16)
    r = x - hi.astype(F32)
    mid = r.astype(BF16)
    lo = (r - mid.astype(F32)).astype(BF16)
    return hi, mid, lo


def _dot_x2(x, m):
    hi, lo = _split2(x)
    return _dot(hi, m) + _dot(lo, m)


def _head_rms(t, gain, hm):
    outs = []
    for c in range(t.shape[1] // LANES):
        tc = t[:, c * LANES:(c + 1) * LANES]
        ms = _dot_x2(tc * tc, hm)
        outs.append(tc * lax.rsqrt(ms + RMS_EPS) * gain)
    return outs[0] if len(outs) == 1 else jnp.concatenate(outs, axis=1)


def _head_mean_matrix():
    i = np.arange(LANES)
    return jnp.asarray((i[:, None] // HEAD_DIM == i[None, :] // HEAD_DIM) / HEAD_DIM, BF16)


_FQ0, _NQ0, _FK0, _FV0, _KV0, _MISC0, _PROJ_COLS = 0, 1024, 2048, 2560, 3072, 3840, 3968


def _proj_weights(w_in, b_fox_f, b_nsa_gate):
    fw = N_FOX_HEADS * HEAD_DIM
    kvw = N_NSA_KV * HEAD_DIM
    o = 0
    fq = w_in[:, o:o + fw]; o += fw
    fk = w_in[:, o:o + fw]; o += fw
    fv = w_in[:, o:o + fw]; o += fw
    ff = w_in[:, o:o + N_FOX_HEADS]; o += N_FOX_HEADS
    nq = w_in[:, o:o + fw]; o += fw
    kv = w_in[:, o:o + 6 * kvw]; o += 6 * kvw
    ng = w_in[:, o:o + 3 * N_NSA_HEADS]
    d = w_in.shape[0]
    z64 = jnp.zeros((d, HEAD_DIM), w_in.dtype)
    fq_x, nq_x = [], []
    for h in range(N_FOX_HEADS):
        qh = fq[:, h * HEAD_DIM:(h + 1) * HEAD_DIM]
        fq_x += [qh, z64] if h % 2 == 0 else [z64, qh]
    for h in range(N_NSA_HEADS):
        qh = nq[:, h * HEAD_DIM:(h + 1) * HEAD_DIM]
        fq_pos = h // NSA_GROUP
        nq_x += [qh, z64] if fq_pos == 0 else [z64, qh]
    misc = jnp.concatenate([ff, ng, jnp.zeros((d, LANES - N_FOX_HEADS - 3 * N_NSA_HEADS), w_in.dtype)], axis=1)
    w = jnp.concatenate(fq_x + nq_x + [fk, fv, kv, misc], axis=1).astype(BF16)
    bias = jnp.concatenate([b_fox_f, b_nsa_gate,
                            jnp.zeros((LANES - N_FOX_HEADS - 3 * N_NSA_HEADS,), F32)]).reshape(1, LANES)
    return w, bias


def _bf16_pieces(x, n):
    out, r = [], float(x)
    for _ in range(n):
        p = float(np.asarray(r, F32).astype(BF16))
        out.append(p)
        r -= p
    return out


def _query_constants():
    fox = np.zeros((N_FOX_HEADS, LANES), np.float32)
    nsa = np.zeros((N_NSA_HEADS, LANES), np.float32)
    for h in range(N_FOX_HEADS):
        spare = HEAD_DIM * (1 - h % 2)
        fox[h, spare:spare + N_BIAS_LANES] = 1.0
    for h in range(N_NSA_HEADS):
        spare = HEAD_DIM * (1 - h // NSA_GROUP)
        for i, piece in enumerate(_bf16_pieces(_slope(h) * LOG2E, N_BIAS_LANES)):
            nsa[h, spare + 2 * i:spare + 2 * i + 2] = piece
    return jnp.asarray(np.stack([fox.reshape(-1), nsa.reshape(-1)]))


def _proj_kernel(x_ref, g_ref, w_ref, bias_ref, gains_ref, hm_ref, qc_ref,
                 fq_ref, nq_ref, fk_ref, fv_ref, ck_ref, cv_ref, sk_ref, sv_ref, wk_ref, wv_ref, misc_ref):
    x = x_ref[...]
    ms = jnp.mean(x * x, axis=-1, keepdims=True)
    xn = (x * lax.rsqrt(ms + RMS_EPS)) * g_ref[...]
    z = _dot(xn.astype(BF16), w_ref[...])
    hm = hm_ref[...]
    scale = HEAD_DIM ** -0.5 * LOG2E
    fq_ref[...] = (_head_rms(z[:, _FQ0:_NQ0], gains_ref[0:1, :], hm) * scale + qc_ref[0:1, :]).astype(BF16)
    nq_ref[...] = (_head_rms(z[:, _NQ0:_FK0], gains_ref[1:2, :], hm) * scale + qc_ref[1:2, :]).astype(BF16)
    fk_ref[...] = _head_rms(z[:, _FK0:_FV0], gains_ref[2:3, :], hm)
    fv_ref[...] = z[:, _FV0:_KV0]
    kv = z[:, _KV0:_MISC0]
    ck_ref[...] = kv[:, 0:128]
    cv_ref[...] = kv[:, 128:256]
    sk_ref[...] = _head_rms(kv[:, 256:384], gains_ref[3:4, :], hm)
    sv_ref[...] = kv[:, 384:512]
    wk_ref[...] = _head_rms(kv[:, 512:640], gains_ref[4:5, :], hm)
    wv_ref[...] = kv[:, 640:768]
    zb = z[:, _MISC0:_PROJ_COLS] + bias_ref[...]
    soft = jnp.log1p(jnp.exp(-jnp.abs(zb)))
    logsig = jnp.minimum(zb, 0.0) - soft
    lane = lax.broadcasted_iota(jnp.int32, zb.shape, 1)
    misc_ref[...] = jnp.where(lane < N_FOX_HEADS, logsig, jnp.exp(logsig))


def _project(x2d, g_attn, w, bias, gains, hm, tm):
    n, d = x2d.shape
    row = lambda c: pl.BlockSpec((tm, c), lambda i: (i, 0))
    full = lambda a: pl.BlockSpec(a.shape, lambda i: (0,) * a.ndim)
    g2 = g_attn.reshape(1, d)
    qc = _query_constants()
    out_shapes = ([jax.ShapeDtypeStruct((n, 1024), BF16)] * 2 + [jax.ShapeDtypeStruct((n, 512), F32)] * 2
                  + [jax.ShapeDtypeStruct((n, LANES), F32)] * 7)
    out_specs = [row(1024), row(1024), row(512), row(512)] + [row(LANES)] * 7
    return pl.pallas_call(
        _proj_kernel, grid=(n // tm,),
        in_specs=[row(d), full(g2), full(w), full(bias), full(gains), full(hm), full(qc)],
        out_specs=out_specs, out_shape=out_shapes,
        compiler_params=_cparams(("parallel",)), name="proj",
    )(x2d, g2, w, bias, gains, hm, qc)


def _dot_x3(x, m):
    hi, mid, lo = _split3(x)
    return _dot(hi, m) + _dot(mid, m) + _dot(lo, m)


def _cumsum_kernel(x_ref, u_ref, sl_ref, o_ref):
    local = _dot_x3(x_ref[0], u_ref[...])
    hi, mid, lo = _split3(jnp.broadcast_to(local[:, LANES - 1:LANES], local.shape))
    sl = sl_ref[...]
    o_ref[0] = local + (_dot(sl, hi) + _dot(sl, mid) + _dot(sl, lo))


def _cumsum_lanes(xt):
    b, h, s = xt.shape
    nchunk = s // LANES
    rows = h * nchunk
    i = np.arange(LANES)
    u = jnp.asarray(i[:, None] <= i[None, :], BF16)
    r = np.arange(rows)
    sl = jnp.asarray((r[:, None] // nchunk == r[None, :] // nchunk) & (r[None, :] < r[:, None]), BF16)
    spec = pl.BlockSpec((1, rows, LANES), lambda i: (i, 0, 0))
    out = pl.pallas_call(
        _cumsum_kernel, grid=(b,),
        in_specs=[spec, pl.BlockSpec((LANES, LANES), lambda i: (0, 0)), pl.BlockSpec((rows, rows), lambda i: (0, 0))],
        out_specs=spec, out_shape=jax.ShapeDtypeStruct((b, rows, LANES), F32),
        compiler_params=_cparams(("parallel",)), name="cumsum",
    )(xt.reshape(b, rows, LANES), u, sl)
    return out.reshape(b, h, s)


def _tri_pairs(n):
    qi = np.concatenate([np.full(i + 1, i) for i in range(n)]).astype(np.int32)
    ki = np.concatenate([np.arange(i + 1) for i in range(n)]).astype(np.int32)
    return jnp.asarray(qi), jnp.asarray(ki)


def _softmax_step(s, v, m_ref, l_ref, acc_ref, idx):
    m_prev = m_ref[idx]
    m_new = jnp.maximum(m_prev, jnp.max(s, axis=1, keepdims=True))
    alpha = jnp.exp(m_prev - m_new)
    p = jnp.exp(s - m_new)
    l_ref[idx] = alpha * l_ref[idx] + jnp.sum(p, axis=1, keepdims=True)
    acc_ref[idx] = alpha * acc_ref[idx] + _dot(p.astype(BF16), v)
    m_ref[idx] = m_new


def _init_softmax(m_ref, l_ref, acc_ref):
    m_ref[...] = jnp.full(m_ref.shape, NEG, F32)
    l_ref[...] = jnp.zeros(l_ref.shape, F32)
    acc_ref[...] = jnp.zeros(acc_ref.shape, F32)


def _exp2_step(s, v, m_ref, l_ref, acc_ref, idx):
    cols = s.shape[1] // LANES
    m_prev = m_ref[idx]
    m_new = jnp.maximum(m_prev, jnp.max(s, axis=1, keepdims=True))
    alpha = jnp.exp2(m_prev - m_new)
    p = jnp.exp2(s - jnp.tile(m_new, (1, cols)))
    l_ref[idx] = alpha * l_ref[idx] + sum(p[:, c * LANES:(c + 1) * LANES] for c in range(cols))
    acc_ref[idx] = alpha * acc_ref[idx] + _dot(p.astype(BF16), v)
    m_ref[idx] = m_new


def _softmax_result(l_ref, acc_ref, idx):
    return acc_ref[idx] / jnp.sum(l_ref[idx], axis=1, keepdims=True)


def _lower_tri(t):
    return lax.broadcasted_iota(jnp.int32, (t, t), 1) <= lax.broadcasted_iota(jnp.int32, (t, t), 0)


def _fox_bias_kernel(x_ref, lt_ref, sl_ref, pm_ref, a_ref, c_ref, tot_ref):
    nchunk = tot_ref.shape[0]
    lt = lt_ref[...]

    def local(i, carry):
        rows = pl.ds(pl.multiple_of(i * LANES, LANES), LANES)
        hi, mid, lo = _split3(x_ref[0, rows, :])
        c = _dot(lt, hi) + _dot(lt, mid) + _dot(lt, lo)
        c_ref[rows, :] = c
        tot_ref[pl.ds(i, 1), :] = c[LANES - 1:LANES, :]
        return carry

    lax.fori_loop(0, nchunk, local, 0)
    hi, mid, lo = _split3(tot_ref[...])
    sl = sl_ref[...]
    tot_ref[...] = _dot(sl, hi) + _dot(sl, mid) + _dot(sl, lo)

    def place(i, carry):
        rows = pl.ds(pl.multiple_of(i * LANES, LANES), LANES)
        hi, mid, lo = _split3((c_ref[rows, :] + tot_ref[pl.ds(i, 1), :]) * (-LOG2E))
        for hp in range(a_ref.shape[1]):
            a = _dot(hi, pm_ref[hp, 0]) + _dot(mid, pm_ref[hp, 1]) + _dot(lo, pm_ref[hp, 2])
            a_ref[0, hp, rows, :] = a.astype(BF16)
        return carry

    lax.fori_loop(0, nchunk, place, 0)


def _fox_bias(misc):
    b, s, _ = misc.shape
    nchunk = s // LANES
    i = np.arange(LANES)
    lt = jnp.asarray(i[:, None] >= i[None, :], BF16)
    c = np.arange(nchunk)
    sl = jnp.asarray(c[:, None] > c[None, :], BF16)
    pm = np.zeros((N_FOX_HEADS // 2, N_BIAS_LANES, LANES, LANES), np.float32)
    for hp in range(N_FOX_HEADS // 2):
        for piece in range(N_BIAS_LANES):
            pm[hp, piece, 2 * hp, HEAD_DIM + piece] = 1.0
            pm[hp, piece, 2 * hp + 1, piece] = 1.0
    pm = jnp.asarray(pm, BF16)
    full = lambda a: pl.BlockSpec(a.shape, lambda i: (0,) * a.ndim)
    return pl.pallas_call(
        _fox_bias_kernel, grid=(b,),
        in_specs=[pl.BlockSpec((1, s, LANES), lambda i: (i, 0, 0)), full(lt), full(sl), full(pm)],
        out_specs=pl.BlockSpec((1, N_FOX_HEADS // 2, s, LANES), lambda i: (i, 0, 0, 0)),
        out_shape=jax.ShapeDtypeStruct((b, N_FOX_HEADS // 2, s, LANES), BF16),
        scratch_shapes=[pltpu.VMEM((s, LANES), F32), pltpu.VMEM((nchunk, LANES), F32)],
        compiler_params=_cparams(("parallel",)), name="fox_bias",
    )(misc, lt, sl, pm)


def _fox_kernel(qt_ref, kt_ref, q_ref, k_ref, v_ref, a_ref, o_ref, m_ref, l_ref, acc_ref, *, t):
    step = pl.program_id(2)
    qi = qt_ref[step]
    ki = kt_ref[step]

    @pl.when(ki == 0)
    def _():
        _init_softmax(m_ref, l_ref, acc_ref)

    def update(diag):
        k = k_ref[0].astype(BF16)
        v = v_ref[0].astype(BF16)
        a = a_ref[0, 0]
        lane = lax.broadcasted_iota(jnp.int32, k.shape, 1)
        keys = (jnp.where(lane < HEAD_DIM, k, a), jnp.where(lane < HEAD_DIM, a, k))
        for h in range(2):
            s = _dot_nt(q_ref[0, :, h * LANES:(h + 1) * LANES], keys[h])
            if diag:
                s = jnp.where(_lower_tri(t), s, NEG)
            _exp2_step(s, v, m_ref, l_ref, acc_ref, h)

    @pl.when(ki < qi)
    def _():
        update(False)

    @pl.when(ki == qi)
    def _():
        update(True)
        lane = lax.broadcasted_iota(jnp.int32, (t, LANES), 1)
        o = jnp.where(lane < HEAD_DIM, _softmax_result(l_ref, acc_ref, 0), _softmax_result(l_ref, acc_ref, 1))
        o_ref[0] = o.astype(BF16)


def _fox_prompt(fq_x, fk, fv, bias, t):
    b, s, _ = fk.shape
    qt, kt = _tri_pairs(s // t)
    grid_spec = pltpu.PrefetchScalarGridSpec(
        num_scalar_prefetch=2, grid=(b, N_FOX_HEADS // 2, qt.shape[0]),
        in_specs=[
            pl.BlockSpec((1, t, 2 * LANES), lambda b, hp, i, qt, kt: (b, qt[i], hp)),
            pl.BlockSpec((1, t, LANES), lambda b, hp, i, qt, kt: (b, kt[i], hp)),
            pl.BlockSpec((1, t, LANES), lambda b, hp, i, qt, kt: (b, kt[i], hp)),
            pl.BlockSpec((1, 1, t, LANES), lambda b, hp, i, qt, kt: (b, hp, kt[i], 0)),
        ],
        out_specs=pl.BlockSpec((1, t, LANES), lambda b, hp, i, qt, kt: (b, qt[i], hp)),
        scratch_shapes=[pltpu.VMEM((2, t, LANES), F32)] * 3,
    )
    return pl.pallas_call(
        functools.partial(_fox_kernel, t=t), grid_spec=grid_spec,
        out_shape=jax.ShapeDtypeStruct((b, s, 512), BF16),
        compiler_params=_cparams(("parallel", "parallel", "arbitrary")), name="fox_prompt",
    )(qt, kt, fq_x, fk, fv, bias)


def _cmp_weights(w_cmp, pe):
    wl = w_cmp.reshape(CMP_BLOCK, HEAD_DIM, HEAD_DIM)
    eye = jnp.eye(N_NSA_KV, dtype=w_cmp.dtype)
    half = CMP_BLOCK // 2
    w2 = jnp.einsum('lde,gh->lgdhe', wl, eye).reshape(2, half * N_NSA_KV * HEAD_DIM, N_NSA_KV * HEAD_DIM)
    pe2 = jnp.tile(pe.reshape(2, half, 1, HEAD_DIM), (1, 1, N_NSA_KV, 1)).reshape(2, half * N_NSA_KV * HEAD_DIM)
    return w2.astype(BF16), pe2


def _compress(x, w_ref, pe_ref, i0):
    n = x.shape[0]
    a = _dot((x + pe_ref[i0:i0 + 1, :]).astype(BF16), w_ref[i0])
    b = _dot((x + pe_ref[i0 + 1:i0 + 2, :]).astype(BF16), w_ref[i0 + 1])
    return a + pltpu.roll(b, n - 1, axis=0)


def _cmp_kernel(xk_ref, xv_ref, w_ref, pe_ref, g_ref, hm_ref, kc_ref, vc_ref):
    kc = _compress(xk_ref[0], w_ref, pe_ref, 0)
    kc_ref[0] = _head_rms(kc, g_ref[...], hm_ref[...]).astype(BF16)
    vc_ref[0] = _compress(xv_ref[0], w_ref, pe_ref, 2).astype(BF16)


def _cmp_prompt(ck, cv, w4, pe4, g_k_cmp2, hm):
    b, s, _ = ck.shape
    n = s // CMP_STRIDE
    wide = CMP_STRIDE * LANES
    xspec = pl.BlockSpec((1, n, wide), lambda i: (i, 0, 0))
    full = lambda a: pl.BlockSpec(a.shape, lambda i: (0,) * a.ndim)
    ospec = pl.BlockSpec((1, n, LANES), lambda i: (i, 0, 0))
    return pl.pallas_call(
        _cmp_kernel, grid=(b,),
        in_specs=[xspec, xspec, full(w4), full(pe4), full(g_k_cmp2), full(hm)],
        out_specs=[ospec, ospec], out_shape=[jax.ShapeDtypeStruct((b, n, LANES), BF16)] * 2,
        compiler_params=_cparams(("parallel",)), name="cmp_prompt",
    )(ck.reshape(b, n, wide), cv.reshape(b, n, wide), w4, pe4, g_k_cmp2, hm)


def _slope(h):
    return float(2.0 ** (-8.0 * (h + 1) / N_NSA_HEADS))


def _overlap_matrix(nc, nsb):
    c0 = np.arange(nc)[:, None] * CMP_STRIDE
    s0 = np.arange(nsb)[None, :] * SEL_BLOCK
    return jnp.asarray((c0 < s0 + SEL_BLOCK) & (c0 + CMP_BLOCK > s0), BF16)


def _select_blocks(imp, t, nsb):
    jb = lax.broadcasted_iota(jnp.int32, imp.shape, 1)
    cur = t // SEL_BLOCK
    visible = jb * SEL_BLOCK <= t
    work = jnp.where((jb == 0) | (jb == cur) | (jb == cur - 1), FORCE_SCORE, imp)
    work = jnp.where(visible, work, NEG)
    jbf = jb.astype(F32)
    sel = jnp.zeros(imp.shape, F32)
    for _ in range(min(N_SELECT, nsb)):
        mx = jnp.max(work, axis=1, keepdims=True)
        first = jnp.min(jnp.where(work == mx, jbf, float(nsb)), axis=1, keepdims=True)
        pick = jbf == first
        sel = jnp.where(pick, 1.0, sel)
        work = jnp.where(pick, 2.0 * NEG, work)
    return jnp.where(visible, sel, 0.0)


def _cmp_attend(q, kc, vc, d, slope):
    valid = d >= 0.0
    s = jnp.where(valid, _dot_nt(q, kc) * LN2 - slope * d, NEG)
    m = jnp.max(s, axis=1, keepdims=True)
    e = jnp.where(valid, jnp.exp(s - m), 0.0)
    p = e / jnp.maximum(jnp.sum(e, axis=1, keepdims=True), TINY)
    return p, _dot(p.astype(BF16), vc)


def _data_lanes(q, g):
    lane = lax.broadcasted_iota(jnp.int32, q.shape, 1)
    return jnp.where(lane // HEAD_DIM == g, q, jnp.zeros_like(q))


def _nsa_cmp_kernel(q_ref, kc_ref, vc_ref, ov_ref, oc_ref, unsel_ref, *, tq, nc_valid):
    qi = pl.program_id(1)
    ncp = kc_ref.shape[1]
    nsbp = ov_ref.shape[1]
    t = qi * tq + lax.broadcasted_iota(jnp.int32, (tq, 1), 0)
    n = lax.broadcasted_iota(jnp.int32, (1, ncp), 1)
    cend = jnp.where(n < nc_valid, n * CMP_STRIDE + CMP_BLOCK - 1, jnp.int32(2 ** 30))
    d = (t - cend).astype(F32)
    kc = kc_ref[0]
    vc = vc_ref[0]
    ov = ov_ref[...]
    for g in range(N_NSA_KV):
        psum = jnp.zeros((tq, ncp), F32)
        for j in range(NSA_GROUP):
            h = g * NSA_GROUP + j
            p, o = _cmp_attend(_data_lanes(q_ref[0, :, h * LANES:(h + 1) * LANES], g), kc, vc, d, _slope(h))
            oc_ref[0, :, h * LANES:(h + 1) * LANES] = o
            psum = psum + p
        unsel_ref[0, g] = (1.0 - _select_blocks(_dot_x2(psum, ov), t, nsbp)).astype(BF16)


def _nsa_cmp_prompt(nq_x, kc, vc, tq):
    b, s, _ = nq_x.shape
    ncp = kc.shape[1]
    nc_valid = (s - CMP_BLOCK) // CMP_STRIDE + 1
    nsb = -(-s // SEL_BLOCK)
    assert nsb <= LANES, "the block map rides in one 128-lane contraction tile"
    ov = _overlap_matrix(ncp, LANES)
    return pl.pallas_call(
        functools.partial(_nsa_cmp_kernel, tq=tq, nc_valid=nc_valid), grid=(b, s // tq),
        in_specs=[pl.BlockSpec((1, tq, 1024), lambda b, i: (b, i, 0)),
                  pl.BlockSpec((1, ncp, LANES), lambda b, i: (b, 0, 0)),
                  pl.BlockSpec((1, ncp, LANES), lambda b, i: (b, 0, 0)),
                  pl.BlockSpec((ncp, LANES), lambda b, i: (0, 0))],
        out_specs=[pl.BlockSpec((1, tq, 1024), lambda b, i: (b, i, 0)),
                   pl.BlockSpec((1, N_NSA_KV, tq, LANES), lambda b, i: (b, 0, i, 0))],
        out_shape=[jax.ShapeDtypeStruct((b, s, 1024), F32), jax.ShapeDtypeStruct((b, N_NSA_KV, s, LANES), BF16)],
        compiler_params=_cparams(("parallel", "parallel")), name="nsa_cmp_prompt",
    )(nq_x, kc, vc, ov)


def _pack_heads(o_a, o_b, g):
    lane = lax.broadcasted_iota(jnp.int32, o_a.shape, 1)
    first = jnp.where(lane < HEAD_DIM, o_a, pltpu.roll(o_b, HEAD_DIM, axis=1))
    second = jnp.where(lane < HEAD_DIM, pltpu.roll(o_a, HEAD_DIM, axis=1), o_b)
    return jnp.where(g == 0, first, second)


def _nsa_mix_kernel(qt_ref, kt_ref, q_ref, sk_ref, sv_ref, unsel_ref, wkp_ref, wvp_ref, wkc_ref, wvc_ref,
                    oc_ref, misc_ref, o_ref, m_ref, l_ref, acc_ref, *, t):
    g = pl.program_id(1)
    step = pl.program_id(2)
    qi = qt_ref[step]
    ki = kt_ref[step]

    @pl.when(ki == 0)
    def _():
        _init_softmax(m_ref, l_ref, acc_ref)

    def update(diag):
        k = sk_ref[0].astype(BF16)
        v = sv_ref[0].astype(BF16)
        lane = lax.broadcasted_iota(jnp.int32, (t, LANES), 1)
        row = lax.broadcasted_iota(jnp.int32, (t, LANES), 0)
        pos = ki * t + row
        spare = lane - HEAD_DIM * (1 - g)
        pos_lo = pos & 255
        pos_part = jnp.where((spare & 1) == 0, pos - pos_lo, pos_lo).astype(F32).astype(BF16)
        keys = jnp.where((spare >= 0) & (spare < 2 * N_BIAS_LANES), pos_part, k)
        blk = ki * (t // SEL_BLOCK) + row // SEL_BLOCK
        rhs = jnp.concatenate([keys, jnp.where(lane == blk, MASK_SCORE, 0.0).astype(BF16)], axis=1)
        unsel = unsel_ref[0, 0]
        for j in range(NSA_GROUP):
            lhs = jnp.concatenate([q_ref[0, :, j * LANES:(j + 1) * LANES], unsel], axis=1)
            s = _dot_nt(lhs, rhs)
            if diag:
                s = jnp.where(_lower_tri(t), s, NEG)
            _exp2_step(s, v, m_ref, l_ref, acc_ref, j)

    @pl.when(ki < qi)
    def _():
        update(False)

    @pl.when(ki == qi)
    def _():
        update(True)
        kw = jnp.concatenate([wkp_ref[0], wkc_ref[0]], axis=0).astype(BF16)
        vw = jnp.concatenate([wvp_ref[0], wvc_ref[0]], axis=0).astype(BF16)
        nw = WINDOW + t
        dw = WINDOW + lax.broadcasted_iota(jnp.int32, (t, nw), 0) - lax.broadcasted_iota(jnp.int32, (t, nw), 1)
        kp = qi * t - WINDOW + lax.broadcasted_iota(jnp.int32, (1, nw), 1)
        ok = (dw >= 0) & (dw <= WINDOW) & (kp >= 0)
        dwf = dw.astype(F32)
        gates = misc_ref[0]
        outs = []
        for j in range(NSA_GROUP):
            slope = jnp.where(g == 0, _slope(j), _slope(NSA_GROUP + j))
            q = _data_lanes(q_ref[0, :, j * LANES:(j + 1) * LANES], g)
            s = jnp.where(ok, _dot_nt(q, kw) * LN2 - slope * dwf, NEG)
            e = jnp.exp(s - jnp.max(s, axis=1, keepdims=True))
            o_w = _dot(e.astype(BF16), vw) / jnp.sum(e, axis=1, keepdims=True)
            c0 = N_FOX_HEADS + 3 * j
            c1 = c0 + 3 * NSA_GROUP
            gate = [jnp.where(g == 0, gates[:, c0 + c:c0 + c + 1], gates[:, c1 + c:c1 + c + 1]) for c in range(3)]
            outs.append(gate[0] * oc_ref[0, :, j * LANES:(j + 1) * LANES]
                        + gate[1] * _softmax_result(l_ref, acc_ref, j) + gate[2] * o_w)
        for c in range(NSA_GROUP // 2):
            o_ref[0, :, c * LANES:(c + 1) * LANES] = _pack_heads(outs[2 * c], outs[2 * c + 1], g).astype(BF16)


def _nsa_mix_prompt(nq_x, sk, sv, unsel, wk, wv, oc_x, misc, t):
    b, s, _ = nq_x.shape
    assert t % WINDOW == 0
    qt, kt = _tri_pairs(s // t)
    gw = NSA_GROUP * LANES
    qmap = lambda b, g, i, qt, kt: (b, qt[i], 0)
    gmap = lambda b, g, i, qt, kt: (b, qt[i], g)
    kmap = lambda b, g, i, qt, kt: (b, kt[i], 0)
    pmap = lambda b, g, i, qt, kt: (b, jnp.maximum(qt[i] * (t // WINDOW) - 1, 0), 0)
    grid_spec = pltpu.PrefetchScalarGridSpec(
        num_scalar_prefetch=2, grid=(b, N_NSA_KV, qt.shape[0]),
        in_specs=[
            pl.BlockSpec((1, t, gw), gmap),
            pl.BlockSpec((1, t, LANES), kmap), pl.BlockSpec((1, t, LANES), kmap),
            pl.BlockSpec((1, 1, t, LANES), lambda b, g, i, qt, kt: (b, g, qt[i], 0)),
            pl.BlockSpec((1, WINDOW, LANES), pmap), pl.BlockSpec((1, WINDOW, LANES), pmap),
            pl.BlockSpec((1, t, LANES), qmap), pl.BlockSpec((1, t, LANES), qmap),
            pl.BlockSpec((1, t, gw), gmap),
            pl.BlockSpec((1, t, LANES), qmap),
        ],
        out_specs=pl.BlockSpec((1, t, gw // 2), gmap),
        scratch_shapes=[pltpu.VMEM((NSA_GROUP, t, LANES), F32)] * 3,
    )
    return pl.pallas_call(
        functools.partial(_nsa_mix_kernel, t=t), grid_spec=grid_spec,
        out_shape=jax.ShapeDtypeStruct((b, s, 512), BF16),
        compiler_params=_cparams(("parallel", "parallel", "arbitrary")), name="nsa_mix_prompt",
    )(qt, kt, nq_x, sk, sv, unsel, wk, wv, wk, wv, oc_x, misc)


_ROUTE_E, _ROUTE_GATE, _ROUTE_RANK = 0, TOP_K, 2 * TOP_K


def _router_weights(w_router, b_router):
    d = w_router.shape[0]
    wp = jnp.concatenate([w_router, jnp.zeros((d, LANES - N_EXPERTS), F32)], axis=1)
    hi = wp.astype(BF16)
    lo = (wp - hi.astype(F32)).astype(BF16)
    bias = jnp.concatenate([b_router.astype(F32), jnp.full((LANES - N_EXPERTS,), NEG, F32)]).reshape(1, LANES)
    return hi, lo, bias


def _tail_kernel(x_ref, mf_ref, mn_ref, wo_ref, g_ref, wrh_ref, wrl_ref, br_ref, tri_ref,
                 x1_ref, h_ref, route_ref, cnt_ref):
    @pl.when(pl.program_id(0) == 0)
    def _():
        cnt_ref[...] = jnp.zeros(cnt_ref.shape, F32)

    half = mf_ref.shape[1]
    x1 = x_ref[...] + _dot(mf_ref[...], wo_ref[0:half, :]) + _dot(mn_ref[...], wo_ref[half:2 * half, :])
    x1_ref[...] = x1
    h = (x1 * lax.rsqrt(jnp.mean(x1 * x1, axis=-1, keepdims=True) + RMS_EPS)) * g_ref[...]
    h_ref[...] = h
    h_hi, h_lo = _split2(h)
    logits = _dot(h_hi, wrh_ref[...]) + _dot(h_lo, wrh_ref[...]) + _dot(h_hi, wrl_ref[...]) + br_ref[...]
    lane = lax.broadcasted_iota(jnp.int32, logits.shape, 1)
    lanef = lane.astype(F32)
    work = logits
    picks, vals, ids = [], [], []
    for _ in range(TOP_K):
        mx = jnp.max(work, axis=1, keepdims=True)
        first = jnp.min(jnp.where(work == mx, lanef, float(LANES)), axis=1, keepdims=True)
        pick = lanef == first
        work = jnp.where(pick, 2.0 * NEG, work)
        picks.append(pick); vals.append(mx); ids.append(first)
    ex = [jnp.exp(v - vals[0]) for v in vals]
    den = ex[0] + ex[1] + ex[2] + ex[3]
    onehot = jnp.zeros(logits.shape, F32)
    for p in picks:
        onehot = jnp.where(p, 1.0, onehot)
    carry = cnt_ref[0:1, :]
    ranks = _dot(tri_ref[...], onehot.astype(BF16)) + carry
    route = jnp.zeros(logits.shape, F32)
    for k in range(TOP_K):
        rank_k = jnp.sum(jnp.where(picks[k], ranks, 0.0), axis=1, keepdims=True)
        route = jnp.where(lane == _ROUTE_E + k, ids[k], route)
        route = jnp.where(lane == _ROUTE_GATE + k, ex[k] / den, route)
        route = jnp.where(lane == _ROUTE_RANK + k, rank_k, route)
    route_ref[...] = route
    cnt_ref[...] = jnp.broadcast_to(carry + jnp.sum(onehot, axis=0, keepdims=True), cnt_ref.shape)


def _tail(x2d, mf, mn, wo, g_ffn, wrh, wrl, br, tm):
    n, d = x2d.shape
    i = np.arange(tm)
    tri = jnp.asarray(i[:, None] > i[None, :], BF16)
    row = lambda c: pl.BlockSpec((tm, c), lambda i: (i, 0))
    full = lambda a: pl.BlockSpec(a.shape, lambda i: (0,) * a.ndim)
    g2 = g_ffn.reshape(1, d)
    return pl.pallas_call(
        _tail_kernel, grid=(n // tm,),
        in_specs=[row(d), row(mf.shape[1]), row(mn.shape[1]), full(wo), full(g2), full(wrh), full(wrl), full(br),
                  full(tri)],
        out_specs=[row(d), row(d), row(LANES), pl.BlockSpec((8, LANES), lambda i: (0, 0))],
        out_shape=[jax.ShapeDtypeStruct((n, d), F32), jax.ShapeDtypeStruct((n, d), F32),
                   jax.ShapeDtypeStruct((n, LANES), F32), jax.ShapeDtypeStruct((8, LANES), F32)],
        compiler_params=_cparams(("arbitrary",)), name="tail",
    )(x2d, mf, mn, wo, g2, wrh, wrl, br, tri)


def _row_copy(src_hbm, src_row, dst, dst_row, sem):
    return pltpu.make_async_copy(src_hbm.at[pl.ds(src_row, 1)], dst.at[pl.ds(dst_row, 1)], sem)


def _dispatch_kernel(dest_ref, h_ref, xs_in_hbm, xs_hbm, sem, *, tm):
    del xs_in_hbm

    def issue(r, c):
        for k in range(TOP_K):
            _row_copy(h_ref, r, xs_hbm, dest_ref[0, 0, TOP_K * r + k], sem).start()
        return c

    def drain(r, c):
        for k in range(TOP_K):
            _row_copy(h_ref, r, xs_hbm, dest_ref[0, 0, TOP_K * r + k], sem).wait()
        return c

    lax.fori_loop(0, tm, issue, 0)
    lax.fori_loop(0, tm, drain, 0)


def _dispatch(h, dest, xs, tm):
    n, d = h.shape
    dest3 = dest.reshape(n // tm, 1, tm * TOP_K)
    return pl.pallas_call(
        functools.partial(_dispatch_kernel, tm=tm), grid=(n // tm,),
        in_specs=[pl.BlockSpec((1, 1, tm * TOP_K), lambda i: (i, 0, 0), memory_space=pltpu.SMEM),
                  pl.BlockSpec((tm, d), lambda i: (i, 0)), pl.BlockSpec(memory_space=pl.ANY)],
        out_specs=pl.BlockSpec(memory_space=pl.ANY),
        out_shape=jax.ShapeDtypeStruct(xs.shape, xs.dtype),
        scratch_shapes=[pltpu.SemaphoreType.DMA(())],
        input_output_aliases={2: 0},
        compiler_params=_cparams(("arbitrary",)), name="moe_dispatch",
    )(dest3, h, xs)


_PAIR = 2 * LANES


def _split_gate_up_kernel(w_ref, p_ref, g_ref, u_ref):
    p = p_ref[...]
    for c in range(w_ref.shape[2] // _PAIR):
        t = _dot(w_ref[0, :, c * _PAIR:(c + 1) * _PAIR].astype(BF16), p)
        g_ref[0, :, c * LANES:(c + 1) * LANES] = t[:, :LANES].astype(BF16)
        u_ref[0, :, c * LANES:(c + 1) * LANES] = t[:, LANES:].astype(BF16)


def _split_gate_up(w_gate_up, rows):
    e, d, de2 = w_gate_up.shape
    perm = np.zeros((_PAIR, _PAIR), np.float32)
    j = np.arange(LANES)
    perm[2 * j, j] = 1.0
    perm[2 * j + 1, LANES + j] = 1.0
    ospec = pl.BlockSpec((1, rows, de2 // 2), lambda i, r: (i, r, 0))
    return pl.pallas_call(
        _split_gate_up_kernel, grid=(e, d // rows),
        in_specs=[pl.BlockSpec((1, rows, de2), lambda i, r: (i, r, 0)),
                  pl.BlockSpec((_PAIR, _PAIR), lambda i, r: (0, 0))],
        out_specs=[ospec, ospec], out_shape=[jax.ShapeDtypeStruct((e, d, de2 // 2), BF16)] * 2,
        compiler_params=_cparams(("parallel", "parallel")), name="split_gate_up",
    )(w_gate_up, jnp.asarray(perm, BF16))


def _expert_kernel(be_ref, nused_ref, x_ref, wg_ref, wu_ref, bg_ref, bu_ref, wd_ref, bd_ref, y_ref):
    @pl.when(pl.program_id(0) < nused_ref[0])
    def _():
        x = x_ref[...].astype(BF16)
        g = jnp.minimum(_dot(x, wg_ref[0]) + bg_ref[0], SWIGLU_LIMIT)
        u = jnp.clip(_dot(x, wu_ref[0]) + bu_ref[0], -SWIGLU_LIMIT, SWIGLU_LIMIT)
        act = (u + 1.0) * g * jax.nn.sigmoid(SWIGLU_ALPHA * g)
        y_ref[...] = _dot(act.astype(BF16), wd_ref[0]) + bd_ref[0]

    @pl.when(pl.program_id(0) >= nused_ref[0])
    def _():
        y_ref[...] = jnp.zeros(y_ref.shape, F32)


def _experts(xs, blk_expert, nused, wg, wu, bg, bu, wd, bd, mb):
    n_slots, d = xs.shape
    de = wg.shape[2]
    rows = lambda i, be, nu: (jnp.minimum(i, nu[0] - 1), 0)
    wmap = lambda i, be, nu: (be[i], 0, 0)
    grid_spec = pltpu.PrefetchScalarGridSpec(
        num_scalar_prefetch=2, grid=(n_slots // mb,),
        in_specs=[pl.BlockSpec((mb, d), rows),
                  pl.BlockSpec((1, d, de), wmap), pl.BlockSpec((1, d, de), wmap),
                  pl.BlockSpec((1, 1, de), wmap), pl.BlockSpec((1, 1, de), wmap),
                  pl.BlockSpec((1, de, d), wmap), pl.BlockSpec((1, 1, d), wmap)],
        out_specs=pl.BlockSpec((mb, d), lambda i, be, nu: (i, 0)),
    )
    return pl.pallas_call(
        _expert_kernel, grid_spec=grid_spec, out_shape=jax.ShapeDtypeStruct((n_slots, d), F32),
        compiler_params=_cparams(("arbitrary",)), name="moe_experts",
    )(blk_expert, nused, xs, wg, wu, bg, bu, wd, bd)


def _combine_kernel(dest_ref, route_ref, x1_ref, ys_hbm, o_ref, buf_ref, sem, *, tm):
    def issue(r, c):
        for k in range(TOP_K):
            _row_copy(ys_hbm, dest_ref[0, 0, TOP_K * r + k], buf_ref.at[k], r, sem).start()
        return c

    def drain(r, c):
        for k in range(TOP_K):
            _row_copy(ys_hbm, dest_ref[0, 0, TOP_K * r + k], buf_ref.at[k], r, sem).wait()
        return c

    lax.fori_loop(0, tm, issue, 0)
    lax.fori_loop(0, tm, drain, 0)
    route = route_ref[...]
    y = route[:, _ROUTE_GATE:_ROUTE_GATE + 1] * buf_ref[0]
    for k in range(1, TOP_K):
        y = y + route[:, _ROUTE_GATE + k:_ROUTE_GATE + k + 1] * buf_ref[k]
    o_ref[...] = x1_ref[...] + y


def _combine(x1, route, dest, ys, tm):
    n, d = x1.shape
    dest3 = dest.reshape(n // tm, 1, tm * TOP_K)
    row = lambda c: pl.BlockSpec((tm, c), lambda i: (i, 0))
    return pl.pallas_call(
        functools.partial(_combine_kernel, tm=tm), grid=(n // tm,),
        in_specs=[pl.BlockSpec((1, 1, tm * TOP_K), lambda i: (i, 0, 0), memory_space=pltpu.SMEM),
                  row(LANES), row(d), pl.BlockSpec(memory_space=pl.ANY)],
        out_specs=row(d), out_shape=jax.ShapeDtypeStruct((n, d), F32),
        scratch_shapes=[pltpu.VMEM((TOP_K, tm, d), F32), pltpu.SemaphoreType.DMA(())],
        compiler_params=_cparams(("arbitrary",)), name="moe_combine",
    )(dest3, route, x1, ys)


def _moe_plan(route_p, cnt_p, route_s, cnt_s, mb):
    n_tok = route_p.shape[0] + route_s.shape[0]
    cp = cnt_p[0, :N_EXPERTS].astype(jnp.int32)
    cs = cnt_s[0, :N_EXPERTS].astype(jnp.int32)
    counts = cp + cs
    padded = (counts + mb - 1) // mb * mb
    pend = jnp.cumsum(padded)
    pstart = pend - padded
    nblk = -(-n_tok * TOP_K // mb) + N_EXPERTS
    blk_expert = jnp.clip(jnp.searchsorted(pend, jnp.arange(nblk, dtype=jnp.int32) * mb, side='right'),
                          0, N_EXPERTS - 1).astype(jnp.int32)
    nused = (pend[-1:] // mb).astype(jnp.int32)

    def dest(route, first):
        e = route[:, _ROUTE_E:_ROUTE_E + TOP_K].astype(jnp.int32)
        rank = route[:, _ROUTE_RANK:_ROUTE_RANK + TOP_K].astype(jnp.int32)
        onehot = e[..., None] == jnp.arange(N_EXPERTS, dtype=jnp.int32)
        return jnp.sum(jnp.where(onehot, first, 0), axis=-1) + rank

    return dest(route_p, pstart), dest(route_s, pstart + cp), blk_expert, nused, nblk


def _page_spec(shape, n_pages, pp, j):
    return pl.BlockSpec((1,) + shape, lambda b, c, pt: (pt[b * n_pages + c * pp + j],) + (0,) * len(shape))


def _slope_col():
    h = lax.broadcasted_iota(jnp.int32, (N_NSA_HEADS, 1), 0).astype(F32)
    return jnp.exp2(-8.0 * (h + 1.0) / N_NSA_HEADS)


def _rounded(ref):
    return ref[0].astype(BF16).astype(F32)


def _append_key(q, s_bias, k_new, v_new, m_ref, l_ref, acc_ref):
    s = jnp.sum(q.astype(F32) * k_new, axis=1, keepdims=True) * LN2 + s_bias
    m_prev = m_ref[0]
    m_new = jnp.maximum(m_prev, s)
    alpha = jnp.exp(m_prev - m_new)
    p = jnp.exp(s - m_new)
    l_ref[0] = alpha * l_ref[0] + p
    acc_ref[0] = alpha * acc_ref[0] + p * v_new
    m_ref[0] = m_new


def _fox_sample_kernel(pt_ref, q_ref, *refs, pp):
    del pt_ref
    k_refs, v_refs = refs[:pp], refs[pp:2 * pp]
    ct_ref, kn_ref, vn_ref, cn_ref, o_ref, m_ref, l_ref, acc_ref = refs[2 * pp:]
    c = pl.program_id(1)

    @pl.when(c == 0)
    def _():
        _init_softmax(m_ref, l_ref, acc_ref)

    q = q_ref[0]
    for j in range(pp):
        s = _dot_nt(q, k_refs[j][0].astype(BF16)) * LN2 - ct_ref[0, :, j * PAGE_SIZE:(j + 1) * PAGE_SIZE]
        _softmax_step(s, v_refs[j][0].astype(BF16), m_ref, l_ref, acc_ref, 0)

    @pl.when(c == pl.num_programs(1) - 1)
    def _():
        _append_key(q, -cn_ref[0][:, 0:1], _rounded(kn_ref), _rounded(vn_ref), m_ref, l_ref, acc_ref)
        o_ref[0] = acc_ref[0] / l_ref[0]


def _fox_sample(pt, qt, cache_k, cache_v, ct, k_new, v_new, c_new, pp):
    b, n_pages = pt.shape
    w = cache_k.shape[2]
    bmap = lambda b, c, pt: (b, 0, 0)
    grid_spec = pltpu.PrefetchScalarGridSpec(
        num_scalar_prefetch=1, grid=(b, n_pages // pp),
        in_specs=([pl.BlockSpec((1, N_FOX_HEADS, w), bmap)]
                  + [_page_spec((PAGE_SIZE, w), n_pages, pp, j) for j in range(pp)] * 2
                  + [pl.BlockSpec((1, N_FOX_HEADS, pp * PAGE_SIZE), lambda b, c, pt: (b, 0, c)),
                     pl.BlockSpec((1, 1, w), bmap), pl.BlockSpec((1, 1, w), bmap),
                     pl.BlockSpec((1, N_FOX_HEADS, LANES), bmap)]),
        out_specs=pl.BlockSpec((1, N_FOX_HEADS, w), bmap),
        scratch_shapes=[pltpu.VMEM((1, N_FOX_HEADS, 1), F32), pltpu.VMEM((1, N_FOX_HEADS, 1), F32),
                        pltpu.VMEM((1, N_FOX_HEADS, w), F32)],
    )
    return pl.pallas_call(
        functools.partial(_fox_sample_kernel, pp=pp), grid_spec=grid_spec,
        out_shape=jax.ShapeDtypeStruct((b, N_FOX_HEADS, w), F32),
        compiler_params=_cparams(("parallel", "arbitrary")), name="fox_sample",
    )(pt.reshape(-1), qt, *([cache_k] * pp), *([cache_v] * pp), ct, k_new, v_new, c_new)


def _cmp_sample_kernel(pt_ref, q_ref, *refs, pp, past):
    del pt_ref
    xk_refs, xv_refs = refs[:pp], refs[pp:2 * pp]
    w_ref, pe_ref, g_ref, hm_ref, ov_ref, oc_ref, sel_ref, a_ref = refs[2 * pp:]
    c = pl.program_id(1)
    rpp = PAGE_SIZE // CMP_STRIDE
    xk = jnp.concatenate([r[0] for r in xk_refs], axis=0)
    xv = jnp.concatenate([r[0] for r in xv_refs], axis=0)
    rows = pl.ds(pl.multiple_of(c * (rpp * pp), rpp * pp), rpp * pp)
    for i, x in ((0, xk), (1, xk), (2, xv), (3, xv)):
        a_ref[i, rows, :] = _dot((x + pe_ref[i:i + 1, :]).astype(BF16), w_ref[i])

    @pl.when(c == pl.num_programs(1) - 1)
    def _():
        n = a_ref.shape[1]
        nsbp = ov_ref.shape[1]
        kc = _head_rms(a_ref[0] + pltpu.roll(a_ref[1], n - 1, axis=0), g_ref[...], hm_ref[...]).astype(BF16)
        vc = (a_ref[2] + pltpu.roll(a_ref[3], n - 1, axis=0)).astype(BF16)
        nc_valid = (past + 1 - CMP_BLOCK) // CMP_STRIDE + 1
        idx = lax.broadcasted_iota(jnp.int32, (N_NSA_HEADS, n), 1)
        cend = jnp.where(idx < nc_valid, idx * CMP_STRIDE + CMP_BLOCK - 1, jnp.int32(2 ** 30))
        p, o = _cmp_attend(q_ref[0], kc, vc, (past - cend).astype(F32), _slope_col())
        oc_ref[0] = o
        psum = jnp.concatenate([jnp.sum(p[0:NSA_GROUP], axis=0, keepdims=True),
                                jnp.sum(p[NSA_GROUP:2 * NSA_GROUP], axis=0, keepdims=True),
                                jnp.zeros((N_NSA_HEADS - N_NSA_KV, n), F32)], axis=0)
        t = jnp.full((N_NSA_HEADS, 1), past, jnp.int32)
        sel = _select_blocks(_dot_x2(psum, ov_ref[...]), t, nsbp)
        sel_ref[0] = jnp.concatenate([jnp.broadcast_to(sel[0:1], (NSA_GROUP, nsbp)),
                                      jnp.broadcast_to(sel[1:2], (NSA_GROUP, nsbp))], axis=0).astype(BF16)


def _cmp_sample(pt, qg, cache_ck, cache_cv, w4, pe4, g_k_cmp2, hm, pp):
    b, n_pages = pt.shape
    past = n_pages * PAGE_SIZE
    wide = cache_ck.shape[2]
    rpp = PAGE_SIZE // CMP_STRIDE
    n = n_pages * rpp
    nsb = -(-(past + 1) // SEL_BLOCK)
    nsbp = -(-nsb // LANES) * LANES
    ov = _overlap_matrix(n, nsbp)
    bmap = lambda b, c, pt: (b, 0, 0)
    full = lambda a: pl.BlockSpec(a.shape, lambda b, c, pt: (0,) * a.ndim)
    grid_spec = pltpu.PrefetchScalarGridSpec(
        num_scalar_prefetch=1, grid=(b, n_pages // pp),
        in_specs=([pl.BlockSpec((1, N_NSA_HEADS, LANES), bmap)]
                  + [_page_spec((rpp, wide), n_pages, pp, j) for j in range(pp)] * 2
                  + [full(w4), full(pe4), full(g_k_cmp2), full(hm), full(ov)]),
        out_specs=[pl.BlockSpec((1, N_NSA_HEADS, LANES), bmap), pl.BlockSpec((1, N_NSA_HEADS, nsbp), bmap)],
        scratch_shapes=[pltpu.VMEM((4, n, LANES), F32)],
    )
    return pl.pallas_call(
        functools.partial(_cmp_sample_kernel, pp=pp, past=past), grid_spec=grid_spec,
        out_shape=[jax.ShapeDtypeStruct((b, N_NSA_HEADS, LANES), F32),
                   jax.ShapeDtypeStruct((b, N_NSA_HEADS, nsbp), BF16)],
        compiler_params=_cparams(("parallel", "arbitrary")), name="cmp_sample",
    )(pt.reshape(-1), qg, *([cache_ck] * pp), *([cache_cv] * pp), w4, pe4, g_k_cmp2, hm, ov)


def _mix_sample_kernel(pt_ref, q_ref, sel_ref, *refs, pp, past):
    del pt_ref
    k_refs, v_refs = refs[:pp], refs[pp:2 * pp]
    (skn_ref, svn_ref, stk_ref, stv_ref, wkn_ref, wvn_ref, oc_ref, gate_ref,
     o_ref, m_ref, l_ref, acc_ref) = refs[2 * pp:]
    c = pl.program_id(1)
    tc = pp * PAGE_SIZE
    nsbp = sel_ref.shape[2]

    @pl.when(c == 0)
    def _():
        _init_softmax(m_ref, l_ref, acc_ref)

    q = q_ref[0]
    slope = _slope_col()
    k = jnp.concatenate([r[0] for r in k_refs], axis=0).astype(BF16)
    v = jnp.concatenate([r[0] for r in v_refs], axis=0).astype(BF16)
    col = lax.broadcasted_iota(jnp.int32, (nsbp, tc), 1)
    blk = lax.broadcasted_iota(jnp.int32, (nsbp, tc), 0)
    expand = (blk == c * (tc // SEL_BLOCK) + col // SEL_BLOCK).astype(BF16)
    chosen = _dot(sel_ref[0], expand) > 0.5
    kpos = (c * tc - past + lax.broadcasted_iota(jnp.int32, (1, tc), 1)).astype(F32)
    s = _dot_nt(q, k) * LN2 + slope * kpos
    _softmax_step(jnp.where(chosen, s, NEG), v, m_ref, l_ref, acc_ref, 0)

    @pl.when(c == pl.num_programs(1) - 1)
    def _():
        _append_key(q, 0.0, _rounded(skn_ref), _rounded(svn_ref), m_ref, l_ref, acc_ref)
        o_s = acc_ref[0] / l_ref[0]
        nbuf = stk_ref.shape[1]
        dwin = (nbuf - lax.broadcasted_iota(jnp.int32, (1, nbuf), 1)).astype(F32)
        s_w = _dot_nt(q, stk_ref[0].astype(BF16)) * LN2 - slope * dwin
        s_n = jnp.sum(q.astype(F32) * _rounded(wkn_ref), axis=1, keepdims=True) * LN2
        mw = jnp.maximum(jnp.max(s_w, axis=1, keepdims=True), s_n)
        e = jnp.exp(s_w - mw)
        e_n = jnp.exp(s_n - mw)
        o_w = ((_dot(e.astype(BF16), stv_ref[0].astype(BF16)) + e_n * _rounded(wvn_ref))
               / (jnp.sum(e, axis=1, keepdims=True) + e_n))
        g = gate_ref[0]
        o_ref[0] = g[:, 0:1] * oc_ref[0] + g[:, 1:2] * o_s + g[:, 2:3] * o_w


def _mix_sample(pt, qg, selh, cache_sk, cache_sv, sk_new, sv_new, st_k, st_v, wk_new, wv_new, oc, gates, pp):
    b, n_pages = pt.shape
    past = n_pages * PAGE_SIZE
    nsbp = selh.shape[2]
    nbuf = st_k.shape[1]
    bmap = lambda b, c, pt: (b, 0, 0)
    one = pl.BlockSpec((1, 1, LANES), bmap)
    grid_spec = pltpu.PrefetchScalarGridSpec(
        num_scalar_prefetch=1, grid=(b, n_pages // pp),
        in_specs=([pl.BlockSpec((1, N_NSA_HEADS, LANES), bmap), pl.BlockSpec((1, N_NSA_HEADS, nsbp), bmap)]
                  + [_page_spec((PAGE_SIZE, LANES), n_pages, pp, j) for j in range(pp)] * 2
                  + [one, one, pl.BlockSpec((1, nbuf, LANES), bmap), pl.BlockSpec((1, nbuf, LANES), bmap), one, one,
                     pl.BlockSpec((1, N_NSA_HEADS, LANES), bmap), pl.BlockSpec((1, N_NSA_HEADS, 3), bmap)]),
        out_specs=pl.BlockSpec((1, N_NSA_HEADS, LANES), bmap),
        scratch_shapes=[pltpu.VMEM((1, N_NSA_HEADS, 1), F32), pltpu.VMEM((1, N_NSA_HEADS, 1), F32),
                        pltpu.VMEM((1, N_NSA_HEADS, LANES), F32)],
    )
    return pl.pallas_call(
        functools.partial(_mix_sample_kernel, pp=pp, past=past), grid_spec=grid_spec,
        out_shape=jax.ShapeDtypeStruct((b, N_NSA_HEADS, LANES), F32),
        compiler_params=_cparams(("parallel", "arbitrary")), name="mix_sample",
    )(pt.reshape(-1), qg, selh, *([cache_sk] * pp), *([cache_sv] * pp), sk_new, sv_new, st_k, st_v,
      wk_new, wv_new, oc, gates)


PROMPT_ROWS = 256
ATTN_TILE = 1024
CMP_Q_TILE = 256
MOE_ROWS = 256
WEIGHT_ROWS = 512
PAGES_PER_STEP = 8


def _layer(xp, xs, c_fox_k, c_fox_v, c_fox_logf, c_cmp_k, c_cmp_v, c_sel_k, c_sel_v, st_win_k, st_win_v, page_table,
           g_attn, w_in, b_fox_f, b_nsa_gate, g_q_fox, g_k_fox, g_q_nsa, g_k_sel, g_k_win, g_k_cmp,
           pe_cmp_k, pe_cmp_v, w_cmp_k, w_cmp_v, w_out, g_ffn, w_router, b_router, w_gate_up, b_gate_up,
           w_down, b_down):
    bp, sp, d = xp.shape
    bs = xs.shape[0]
    n_pool = c_fox_k.shape[0]
    n_pages = page_table.shape[1]
    past = n_pages * PAGE_SIZE

    w, bias = _proj_weights(w_in, b_fox_f, b_nsa_gate)
    tile2 = lambda g: jnp.tile(g, 2)
    gains = jnp.stack([tile2(g_q_fox), tile2(g_q_nsa), tile2(g_k_fox), tile2(g_k_sel), tile2(g_k_win)])
    hm = _head_mean_matrix()
    wk2, pek = _cmp_weights(w_cmp_k, pe_cmp_k)
    wv2, pev = _cmp_weights(w_cmp_v, pe_cmp_v)
    w4 = jnp.concatenate([wk2, wv2])
    pe4 = jnp.concatenate([pek, pev])
    g_cmp2 = tile2(g_k_cmp).reshape(1, LANES)
    wo = w_out.astype(BF16)
    wrh, wrl, br = _router_weights(w_router, b_router)
    wg, wu = _split_gate_up(w_gate_up, WEIGHT_ROWS)
    bg = b_gate_up[:, None, 0::2]
    bu = b_gate_up[:, None, 1::2]
    wd = w_down.astype(BF16)
    bd = b_down[:, None, :]

    pr = _project(xp.reshape(bp * sp, d), g_attn, w, bias, gains, hm, PROMPT_ROWS)
    fq, nq, fk, fv, ck, cv, sk, sv, wk, wv, misc = [a.reshape(bp, sp, -1) for a in pr]
    logf = misc[:, :, :N_FOX_HEADS]
    mix_fox = _fox_prompt(fq, fk, fv, _fox_bias(misc), ATTN_TILE)
    kc, vc = _cmp_prompt(ck, cv, w4, pe4, g_cmp2, hm)
    oc, unsel = _nsa_cmp_prompt(nq, kc, vc, CMP_Q_TILE)
    mix_nsa = _nsa_mix_prompt(nq, sk, sv, unsel, wk, wv, oc, misc, ATTN_TILE)
    x1p, hp, route_p, cnt_p = _tail(xp.reshape(bp * sp, d), mix_fox.reshape(bp * sp, -1),
                                    mix_nsa.reshape(bp * sp, -1), wo, g_ffn, wrh, wrl, br, PROMPT_ROWS)

    sr = _project(xs.reshape(bs, d), g_attn, w, bias, gains, hm, bs)
    fq_s, nq_s, fk_s, fv_s, ck_s, cv_s, sk_s, sv_s, wk_s, wv_s, misc_s = sr
    logf_s = misc_s[:, :N_FOX_HEADS]
    gates_s = misc_s[:, N_FOX_HEADS:N_FOX_HEADS + 3 * N_NSA_HEADS].reshape(bs, N_NSA_HEADS, 3)
    q_heads = fq_s.reshape(bs, N_FOX_HEADS, 2, HEAD_DIM)
    q_heads = jnp.stack([q_heads[:, h, h % 2] for h in range(N_FOX_HEADS)], axis=1)
    qt = jnp.einsum('bhd,hg->bhgd', q_heads, jnp.eye(N_FOX_HEADS, dtype=BF16)).reshape(bs, N_FOX_HEADS, -1)
    qg = nq_s.reshape(bs, N_NSA_HEADS, N_NSA_KV, HEAD_DIM)
    in_group = (jnp.arange(N_NSA_HEADS)[:, None] // NSA_GROUP == jnp.arange(N_NSA_KV)[None, :])[None, :, :, None]
    qg = jnp.where(in_group, qg, jnp.zeros_like(qg)).reshape(bs, N_NSA_HEADS, LANES)
    logf_past = c_fox_logf[page_table].reshape(bs, past, N_FOX_HEADS).astype(F32)
    ct_s = _cumsum_lanes(jnp.transpose(logf_past, (0, 2, 1)))
    c_new = jnp.broadcast_to((ct_s[:, :, -1] + logf_s)[:, :, None], (bs, N_FOX_HEADS, LANES))
    row3 = lambda a: a.reshape(bs, 1, -1)
    o_fox_s = _fox_sample(page_table, qt, c_fox_k.reshape(n_pool, PAGE_SIZE, -1), c_fox_v.reshape(n_pool, PAGE_SIZE, -1),
                          ct_s, row3(fk_s), row3(fv_s), c_new, PAGES_PER_STEP)
    wide = CMP_STRIDE * LANES
    oc_s, selh = _cmp_sample(page_table, qg, c_cmp_k.reshape(n_pool, -1, wide), c_cmp_v.reshape(n_pool, -1, wide),
                             w4, pe4, g_cmp2, hm, PAGES_PER_STEP)
    nbuf = st_win_k.shape[1]
    o_nsa_s = _mix_sample(page_table, qg, selh, c_sel_k.reshape(n_pool, PAGE_SIZE, -1),
                          c_sel_v.reshape(n_pool, PAGE_SIZE, -1), row3(sk_s), row3(sv_s),
                          st_win_k.reshape(bs, nbuf, -1), st_win_v.reshape(bs, nbuf, -1), row3(wk_s), row3(wv_s),
                          oc_s, gates_s, PAGES_PER_STEP)
    o_fox_s = o_fox_s.reshape(bs, N_FOX_HEADS, N_FOX_HEADS, HEAD_DIM)
    mf_s = jnp.stack([o_fox_s[:, h, h] for h in range(N_FOX_HEADS)], axis=1).reshape(bs, -1).astype(BF16)
    o_nsa_s = o_nsa_s.reshape(bs, N_NSA_HEADS, N_NSA_KV, HEAD_DIM)
    mn_s = jnp.stack([o_nsa_s[:, h, h // NSA_GROUP] for h in range(N_NSA_HEADS)], axis=1).reshape(bs, -1).astype(BF16)
    x1s, hs, route_s, cnt_s = _tail(xs.reshape(bs, d), mf_s, mn_s, wo, g_ffn, wrh, wrl, br, bs)

    dest_p, dest_s, blk_expert, nused, nblk = _moe_plan(route_p, cnt_p, route_s, cnt_s, MOE_ROWS)
    slots = jnp.zeros((nblk * MOE_ROWS, d), F32)
    slots = _dispatch(hp, dest_p, slots, PROMPT_ROWS)
    slots = _dispatch(hs, dest_s, slots, bs)
    ys = _experts(slots, blk_expert, nused, wg, wu, bg, bu, wd, bd, MOE_ROWS)
    yp = _combine(x1p, route_p, dest_p, ys, PROMPT_ROWS).reshape(bp, sp, d)
    y_s = _combine(x1s, route_s, dest_s, ys, bs).reshape(bs, 1, d)

    heads = lambda a, b_, s_: a.reshape(b_, s_, -1, HEAD_DIM)
    wp = min(WINDOW, sp)
    rows_p = (heads(fk, bp, sp), heads(fv, bp, sp), logf, heads(ck, bp, sp), heads(cv, bp, sp), heads(sk, bp, sp),
              heads(sv, bp, sp), heads(wk, bp, sp)[:, -wp:], heads(wv, bp, sp)[:, -wp:])
    win_k_s = jnp.concatenate([st_win_k, heads(wk_s, bs, 1)], axis=1)[:, -nbuf:]
    win_v_s = jnp.concatenate([st_win_v, heads(wv_s, bs, 1)], axis=1)[:, -nbuf:]
    rows_s = (heads(fk_s, bs, 1), heads(fv_s, bs, 1), logf_s.reshape(bs, 1, -1), heads(ck_s, bs, 1),
              heads(cv_s, bs, 1), heads(sk_s, bs, 1), heads(sv_s, bs, 1), win_k_s, win_v_s)
    return yp, y_s, rows_p, rows_s


def kernel(x_prompt, x_sample, cache_fox_k, cache_fox_v, cache_fox_logf, cache_cmp_k, cache_cmp_v, cache_sel_k,
           cache_sel_v, state_win_k, state_win_v, page_table, g_attn, w_in, b_fox_f, b_nsa_gate, g_q_fox, g_k_fox,
           g_q_nsa, g_k_sel, g_k_win, g_k_cmp, pe_cmp_k, pe_cmp_v, w_cmp_k, w_cmp_v, w_out, g_ffn, w_router,
           b_router, w_gate_up, b_gate_up, w_down, b_down):
    depth = w_in.shape[0]
    assert depth == 1, "the two groups are threaded through a single layer"
    l = 0
    yp, y_s, rows_p, rows_s = _layer(
        x_prompt, x_sample, cache_fox_k[l], cache_fox_v[l], cache_fox_logf[l], cache_cmp_k[l], cache_cmp_v[l],
        cache_sel_k[l], cache_sel_v[l], state_win_k[l], state_win_v[l], page_table,
        g_attn[l], w_in[l], b_fox_f[l], b_nsa_gate[l], g_q_fox[l], g_k_fox[l], g_q_nsa[l], g_k_sel[l], g_k_win[l],
        g_k_cmp[l], pe_cmp_k[l], pe_cmp_v[l], w_cmp_k[l], w_cmp_v[l], w_out[l], g_ffn[l], w_router[l], b_router[l],
        w_gate_up[l], b_gate_up[l], w_down[l], b_down[l])
    return (yp, y_s) + tuple(r[None] for r in rows_p) + tuple(r[None] for r in rows_s)
```

```python
import functools

import numpy as np
import jax
import jax.numpy as jnp
from jax import lax
from jax.experimental import pallas as pl
from jax.experimental.pallas import tpu as pltpu

F32 = jnp.float32
BF16 = jnp.bfloat16

HEAD_DIM = 64
LANES = 128
N_FOX_HEADS = 8
N_NSA_HEADS = 8
N_NSA_KV = 2
NSA_GROUP = N_NSA_HEADS // N_NSA_KV
CMP_BLOCK = 32
CMP_STRIDE = 16
SEL_BLOCK = 64
N_SELECT = 16
WINDOW = 512
FORCE_SCORE = 1.0e4
N_EXPERTS = 32
TOP_K = 4
SWIGLU_LIMIT = 7.0
SWIGLU_ALPHA = 1.702
RMS_EPS = 1e-6
PAGE_SIZE = 128
NEG = -1.0e30
LOG2E = float(np.log2(np.e))
LN2 = float(np.log(2.0))
N_BIAS_LANES = 3
MASK_SCORE = -float(2 ** 17)
TINY = float(np.finfo(np.float32).tiny)
VMEM_LIMIT = 56 * 1024 * 1024


def _cparams(sem):
    return pltpu.CompilerParams(dimension_semantics=sem, vmem_limit_bytes=VMEM_LIMIT)


def _dot(a, b):
    return jnp.dot(a, b, preferred_element_type=F32)


def _dot_nt(a, b):
    return lax.dot_general(a, b, (((1,), (1,)), ((), ())), preferred_element_type=F32)


def _dot_tn(a, b):
    return lax.dot_general(a, b, (((0,), (0,)), ((), ())), preferred_element_type=F32)


def _split2(x):
    hi = x.astype(BF16)
    lo = (x - hi.astype(F32)).astype(BF16)
    return hi, lo


def _split3(x):
    hi = x.astype(BF16)
    r = x - hi.astype(F32)
    mid = r.astype(BF16)
    lo = (r - mid.astype(F32)).astype(BF16)
    return hi, mid, lo


def _dot_x2(x, m):
    hi, lo = _split2(x)
    return _dot(hi, m) + _dot(lo, m)


def _head_rms(t, gain, hm):
    outs = []
    for c in range(t.shape[1] // LANES):
        tc = t[:, c * LANES:(c + 1) * LANES]
        ms = _dot_x2(tc * tc, hm)
        outs.append(tc * lax.rsqrt(ms + RMS_EPS) * gain)
    return outs[0] if len(outs) == 1 else jnp.concatenate(outs, axis=1)


def _head_mean_matrix():
    i = np.arange(LANES)
    return jnp.asarray((i[:, None] // HEAD_DIM == i[None, :] // HEAD_DIM) / HEAD_DIM, BF16)


_FQ0, _NQ0, _FK0, _FV0, _KV0, _MISC0, _PROJ_COLS = 0, 1024, 2048, 2560, 3072, 3840, 3968


def _proj_weights(w_in, b_fox_f, b_nsa_gate):
    fw = N_FOX_HEADS * HEAD_DIM
    kvw = N_NSA_KV * HEAD_DIM
    o = 0
    fq = w_in[:, o:o + fw]; o += fw
    fk = w_in[:, o:o + fw]; o += fw
    fv = w_in[:, o:o + fw]; o += fw
    ff = w_in[:, o:o + N_FOX_HEADS]; o += N_FOX_HEADS
    nq = w_in[:, o:o + fw]; o += fw
    kv = w_in[:, o:o + 6 * kvw]; o += 6 * kvw
    ng = w_in[:, o:o + 3 * N_NSA_HEADS]
    d = w_in.shape[0]
    z64 = jnp.zeros((d, HEAD_DIM), w_in.dtype)
    fq_x, nq_x = [], []
    for h in range(N_FOX_HEADS):
        qh = fq[:, h * HEAD_DIM:(h + 1) * HEAD_DIM]
        fq_x += [qh, z64] if h % 2 == 0 else [z64, qh]
    for h in range(N_NSA_HEADS):
        qh = nq[:, h * HEAD_DIM:(h + 1) * HEAD_DIM]
        fq_pos = h // NSA_GROUP
        nq_x += [qh, z64] if fq_pos == 0 else [z64, qh]
    misc = jnp.concatenate([ff, ng, jnp.zeros((d, LANES - N_FOX_HEADS - 3 * N_NSA_HEADS), w_in.dtype)], axis=1)
    w = jnp.concatenate(fq_x + nq_x + [fk, fv, kv, misc], axis=1).astype(BF16)
    bias = jnp.concatenate([b_fox_f, b_nsa_gate,
                            jnp.zeros((LANES - N_FOX_HEADS - 3 * N_NSA_HEADS,), F32)]).reshape(1, LANES)
    return w, bias


def _bf16_pieces(x, n):
    out, r = [], float(x)
    for _ in range(n):
        p = float(np.asarray(r, F32).astype(BF16))
        out.append(p)
        r -= p
    return out


def _query_constants():
    fox = np.zeros((N_FOX_HEADS, LANES), np.float32)
    nsa = np.zeros((N_NSA_HEADS, LANES), np.float32)
    for h in range(N_FOX_HEADS):
        spare = HEAD_DIM * (1 - h % 2)
        fox[h, spare:spare + N_BIAS_LANES] = 1.0
    for h in range(N_NSA_HEADS):
        spare = HEAD_DIM * (1 - h // NSA_GROUP)
        for i, piece in enumerate(_bf16_pieces(_slope(h) * LOG2E, N_BIAS_LANES)):
            nsa[h, spare + 2 * i:spare + 2 * i + 2] = piece
    return jnp.asarray(np.stack([fox.reshape(-1), nsa.reshape(-1)]))


def _proj_kernel(x_ref, g_ref, w_ref, bias_ref, gains_ref, hm_ref, qc_ref,
                 fq_ref, nq_ref, fk_ref, fv_ref, ck_ref, cv_ref, sk_ref, sv_ref, wk_ref, wv_ref, misc_ref):
    x = x_ref[...]
    ms = jnp.mean(x * x, axis=-1, keepdims=True)
    xn = (x * lax.rsqrt(ms + RMS_EPS)) * g_ref[...]
    z = _dot(xn.astype(BF16), w_ref[...])
    hm = hm_ref[...]
    scale = HEAD_DIM ** -0.5 * LOG2E
    fq_ref[...] = (_head_rms(z[:, _FQ0:_NQ0], gains_ref[0:1, :], hm) * scale + qc_ref[0:1, :]).astype(BF16)
    nq_ref[...] = (_head_rms(z[:, _NQ0:_FK0], gains_ref[1:2, :], hm) * scale + qc_ref[1:2, :]).astype(BF16)
    fk_ref[...] = _head_rms(z[:, _FK0:_FV0], gains_ref[2:3, :], hm)
    fv_ref[...] = z[:, _FV0:_KV0]
    kv = z[:, _KV0:_MISC0]
    ck_ref[...] = kv[:, 0:128]
    cv_ref[...] = kv[:, 128:256]
    sk_ref[...] = _head_rms(kv[:, 256:384], gains_ref[3:4, :], hm)
    sv_ref[...] = kv[:, 384:512]
    wk_ref[...] = _head_rms(kv[:, 512:640], gains_ref[4:5, :], hm)
    wv_ref[...] = kv[:, 640:768]
    zb = z[:, _MISC0:_PROJ_COLS] + bias_ref[...]
    soft = jnp.log1p(jnp.exp(-jnp.abs(zb)))
    logsig = jnp.minimum(zb, 0.0) - soft
    lane = lax.broadcasted_iota(jnp.int32, zb.shape, 1)
    misc_ref[...] = jnp.where(lane < N_FOX_HEADS, logsig, jnp.exp(logsig))


def _project(x2d, g_attn, w, bias, gains, hm, tm):
    n, d = x2d.shape
    row = lambda c: pl.BlockSpec((tm, c), lambda i: (i, 0))
    full = lambda a: pl.BlockSpec(a.shape, lambda i: (0,) * a.ndim)
    g2 = g_attn.reshape(1, d)
    qc = _query_constants()
    out_shapes = ([jax.ShapeDtypeStruct((n, 1024), BF16)] * 2 + [jax.ShapeDtypeStruct((n, 512), F32)] * 2
                  + [jax.ShapeDtypeStruct((n, LANES), F32)] * 7)
    out_specs = [row(1024), row(1024), row(512), row(512)] + [row(LANES)] * 7
    return pl.pallas_call(
        _proj_kernel, grid=(n // tm,),
        in_specs=[row(d), full(g2), full(w), full(bias), full(gains), full(hm), full(qc)],
        out_specs=out_specs, out_shape=out_shapes,
        compiler_params=_cparams(("parallel",)), name="proj",
    )(x2d, g2, w, bias, gains, hm, qc)


def _dot_x3(x, m):
    hi, mid, lo = _split3(x)
    return _dot(hi, m) + _dot(mid, m) + _dot(lo, m)


def _cumsum_kernel(x_ref, u_ref, sl_ref, o_ref):
    local = _dot_x3(x_ref[0], u_ref[...])
    hi, mid, lo = _split3(jnp.broadcast_to(local[:, LANES - 1:LANES], local.shape))
    sl = sl_ref[...]
    o_ref[0] = local + (_dot(sl, hi) + _dot(sl, mid) + _dot(sl, lo))


def _cumsum_lanes(xt):
    b, h, s = xt.shape
    nchunk = s // LANES
    rows = h * nchunk
    i = np.arange(LANES)
    u = jnp.asarray(i[:, None] <= i[None, :], BF16)
    r = np.arange(rows)
    sl = jnp.asarray((r[:, None] // nchunk == r[None, :] // nchunk) & (r[None, :] < r[:, None]), BF16)
    spec = pl.BlockSpec((1, rows, LANES), lambda i: (i, 0, 0))
    out = pl.pallas_call(
        _cumsum_kernel, grid=(b,),
        in_specs=[spec, pl.BlockSpec((LANES, LANES), lambda i: (0, 0)), pl.BlockSpec((rows, rows), lambda i: (0, 0))],
        out_specs=spec, out_shape=jax.ShapeDtypeStruct((b, rows, LANES), F32),
        compiler_params=_cparams(("parallel",)), name="cumsum",
    )(xt.reshape(b, rows, LANES), u, sl)
    return out.reshape(b, h, s)


def _tri_pairs(n):
    qi = np.concatenate([np.full(i + 1, i) for i in range(n)]).astype(np.int32)
    ki = np.concatenate([np.arange(i + 1) for i in range(n)]).astype(np.int32)
    return jnp.asarray(qi), jnp.asarray(ki)


def _softmax_step(s, v, m_ref, l_ref, acc_ref, idx):
    m_prev = m_ref[idx]
    m_new = jnp.maximum(m_prev, jnp.max(s, axis=1, keepdims=True))
    alpha = jnp.exp(m_prev - m_new)
    p = jnp.exp(s - m_new)
    l_ref[idx] = alpha * l_ref[idx] + jnp.sum(p, axis=1, keepdims=True)
    acc_ref[idx] = alpha * acc_ref[idx] + _dot(p.astype(BF16), v)
    m_ref[idx] = m_new


def _init_softmax(m_ref, l_ref, acc_ref):
    m_ref[...] = jnp.full(m_ref.shape, NEG, F32)
    l_ref[...] = jnp.zeros(l_ref.shape, F32)
    acc_ref[...] = jnp.zeros(acc_ref.shape, F32)


def _exp2_step(s, v, m_ref, l_ref, acc_ref, idx):
    cols = s.shape[1] // LANES
    m_prev = m_ref[idx]
    m_new = jnp.maximum(m_prev, jnp.max(s, axis=1, keepdims=True))
    alpha = jnp.exp2(m_prev - m_new)
    p = jnp.exp2(s - jnp.tile(m_new, (1, cols)))
    l_ref[idx] = alpha * l_ref[idx] + sum(p[:, c * LANES:(c + 1) * LANES] for c in range(cols))
    acc_ref[idx] = alpha * acc_ref[idx] + _dot(p.astype(BF16), v)
    m_ref[idx] = m_new


def _softmax_result(l_ref, acc_ref, idx):
    return acc_ref[idx] / jnp.sum(l_ref[idx], axis=1, keepdims=True)


def _lower_tri(t):
    return lax.broadcasted_iota(jnp.int32, (t, t), 1) <= lax.broadcasted_iota(jnp.int32, (t, t), 0)


def _fox_bias_kernel(x_ref, lt_ref, sl_ref, pm_ref, a_ref, c_ref, tot_ref):
    nchunk = tot_ref.shape[0]
    lt = lt_ref[...]

    def local(i, carry):
        rows = pl.ds(pl.multiple_of(i * LANES, LANES), LANES)
        hi, mid, lo = _split3(x_ref[0, rows, :])
        c = _dot(lt, hi) + _dot(lt, mid) + _dot(lt, lo)
        c_ref[rows, :] = c
        tot_ref[pl.ds(i, 1), :] = c[LANES - 1:LANES, :]
        return carry

    lax.fori_loop(0, nchunk, local, 0)
    hi, mid, lo = _split3(tot_ref[...])
    sl = sl_ref[...]
    tot_ref[...] = _dot(sl, hi) + _dot(sl, mid) + _dot(sl, lo)

    def place(i, carry):
        rows = pl.ds(pl.multiple_of(i * LANES, LANES), LANES)
        hi, mid, lo = _split3((c_ref[rows, :] + tot_ref[pl.ds(i, 1), :]) * (-LOG2E))
        for hp in range(a_ref.shape[1]):
            a = _dot(hi, pm_ref[hp, 0]) + _dot(mid, pm_ref[hp, 1]) + _dot(lo, pm_ref[hp, 2])
            a_ref[0, hp, rows, :] = a.astype(BF16)
        return carry

    lax.fori_loop(0, nchunk, place, 0)


def _fox_bias(misc):
    b, s, _ = misc.shape
    nchunk = s // LANES
    i = np.arange(LANES)
    lt = jnp.asarray(i[:, None] >= i[None, :], BF16)
    c = np.arange(nchunk)
    sl = jnp.asarray(c[:, None] > c[None, :], BF16)
    pm = np.zeros((N_FOX_HEADS // 2, N_BIAS_LANES, LANES, LANES), np.float32)
    for hp in range(N_FOX_HEADS // 2):
        for piece in range(N_BIAS_LANES):
            pm[hp, piece, 2 * hp, HEAD_DIM + piece] = 1.0
            pm[hp, piece, 2 * hp + 1, piece] = 1.0
    pm = jnp.asarray(pm, BF16)
    full = lambda a: pl.BlockSpec(a.shape, lambda i: (0,) * a.ndim)
    return pl.pallas_call(
        _fox_bias_kernel, grid=(b,),
        in_specs=[pl.BlockSpec((1, s, LANES), lambda i: (i, 0, 0)), full(lt), full(sl), full(pm)],
        out_specs=pl.BlockSpec((1, N_FOX_HEADS // 2, s, LANES), lambda i: (i, 0, 0, 0)),
        out_shape=jax.ShapeDtypeStruct((b, N_FOX_HEADS // 2, s, LANES), BF16),
        scratch_shapes=[pltpu.VMEM((s, LANES), F32), pltpu.VMEM((nchunk, LANES), F32)],
        compiler_params=_cparams(("parallel",)), name="fox_bias",
    )(misc, lt, sl, pm)


def _fox_kernel(qt_ref, kt_ref, q_ref, k_ref, v_ref, a_ref, o_ref, m_ref, l_ref, acc_ref, *, t):
    step = pl.program_id(2)
    qi = qt_ref[step]
    ki = kt_ref[step]

    @pl.when(ki == 0)
    def _():
        _init_softmax(m_ref, l_ref, acc_ref)

    def update(diag):
        k = k_ref[0].astype(BF16)
        v = v_ref[0].astype(BF16)
        a = a_ref[0, 0]
        lane = lax.broadcasted_iota(jnp.int32, k.shape, 1)
        keys = (jnp.where(lane < HEAD_DIM, k, a), jnp.where(lane < HEAD_DIM, a, k))
        for h in range(2):
            s = _dot_nt(q_ref[0, :, h * LANES:(h + 1) * LANES], keys[h])
            if diag:
                s = jnp.where(_lower_tri(t), s, NEG)
            _exp2_step(s, v, m_ref, l_ref, acc_ref, h)

    @pl.when(ki < qi)
    def _():
        update(False)

    @pl.when(ki == qi)
    def _():
        update(True)
        lane = lax.broadcasted_iota(jnp.int32, (t, LANES), 1)
        o = jnp.where(lane < HEAD_DIM, _softmax_result(l_ref, acc_ref, 0), _softmax_result(l_ref, acc_ref, 1))
        o_ref[0] = o.astype(BF16)


def _fox_prompt(fq_x, fk, fv, bias, t):
    b, s, _ = fk.shape
    qt, kt = _tri_pairs(s // t)
    grid_spec = pltpu.PrefetchScalarGridSpec(
        num_scalar_prefetch=2, grid=(b, N_FOX_HEADS // 2, qt.shape[0]),
        in_specs=[
            pl.BlockSpec((1, t, 2 * LANES), lambda b, hp, i, qt, kt: (b, qt[i], hp)),
            pl.BlockSpec((1, t, LANES), lambda b, hp, i, qt, kt: (b, kt[i], hp)),
            pl.BlockSpec((1, t, LANES), lambda b, hp, i, qt, kt: (b, kt[i], hp)),
            pl.BlockSpec((1, 1, t, LANES), lambda b, hp, i, qt, kt: (b, hp, kt[i], 0)),
        ],
        out_specs=pl.BlockSpec((1, t, LANES), lambda b, hp, i, qt, kt: (b, qt[i], hp)),
        scratch_shapes=[pltpu.VMEM((2, t, LANES), F32)] * 3,
    )
    return pl.pallas_call(
        functools.partial(_fox_kernel, t=t), grid_spec=grid_spec,
        out_shape=jax.ShapeDtypeStruct((b, s, 512), BF16),
        compiler_params=_cparams(("parallel", "parallel", "arbitrary")), name="fox_prompt",
    )(qt, kt, fq_x, fk, fv, bias)


def _cmp_weights(w_cmp, pe):
    wl = w_cmp.reshape(CMP_BLOCK, HEAD_DIM, HEAD_DIM)
    eye = jnp.eye(N_NSA_KV, dtype=w_cmp.dtype)
    half = CMP_BLOCK // 2
    w2 = jnp.einsum('lde,gh->lgdhe', wl, eye).reshape(2, half * N_NSA_KV * HEAD_DIM, N_NSA_KV * HEAD_DIM)
    pe2 = jnp.tile(pe.reshape(2, half, 1, HEAD_DIM), (1, 1, N_NSA_KV, 1)).reshape(2, half * N_NSA_KV * HEAD_DIM)
    return w2.astype(BF16), pe2


def _compress(x, w_ref, pe_ref, i0):
    n = x.shape[0]
    a = _dot((x + pe_ref[i0:i0 + 1, :]).astype(BF16), w_ref[i0])
    b = _dot((x + pe_ref[i0 + 1:i0 + 2, :]).astype(BF16), w_ref[i0 + 1])
    return a + pltpu.roll(b, n - 1, axis=0)


def _cmp_kernel(xk_ref, xv_ref, w_ref, pe_ref, g_ref, hm_ref, kc_ref, vc_ref):
    kc = _compress(xk_ref[0], w_ref, pe_ref, 0)
    kc_ref[0] = _head_rms(kc, g_ref[...], hm_ref[...]).astype(BF16)
    vc_ref[0] = _compress(xv_ref[0], w_ref, pe_ref, 2).astype(BF16)


def _cmp_prompt(ck, cv, w4, pe4, g_k_cmp2, hm):
    b, s, _ = ck.shape
    n = s // CMP_STRIDE
    wide = CMP_STRIDE * LANES
    xspec = pl.BlockSpec((1, n, wide), lambda i: (i, 0, 0))
    full = lambda a: pl.BlockSpec(a.shape, lambda i: (0,) * a.ndim)
    ospec = pl.BlockSpec((1, n, LANES), lambda i: (i, 0, 0))
    return pl.pallas_call(
        _cmp_kernel, grid=(b,),
        in_specs=[xspec, xspec, full(w4), full(pe4), full(g_k_cmp2), full(hm)],
        out_specs=[ospec, ospec], out_shape=[jax.ShapeDtypeStruct((b, n, LANES), BF16)] * 2,
        compiler_params=_cparams(("parallel",)), name="cmp_prompt",
    )(ck.reshape(b, n, wide), cv.reshape(b, n, wide), w4, pe4, g_k_cmp2, hm)


def _slope(h):
    return float(2.0 ** (-8.0 * (h + 1) / N_NSA_HEADS))


def _overlap_matrix(nc, nsb):
    c0 = np.arange(nc)[:, None] * CMP_STRIDE
    s0 = np.arange(nsb)[None, :] * SEL_BLOCK
    return jnp.asarray((c0 < s0 + SEL_BLOCK) & (c0 + CMP_BLOCK > s0), BF16)


def _select_blocks(imp, t, nsb):
    jb = lax.broadcasted_iota(jnp.int32, imp.shape, 1)
    cur = t // SEL_BLOCK
    visible = jb * SEL_BLOCK <= t
    work = jnp.where((jb == 0) | (jb == cur) | (jb == cur - 1), FORCE_SCORE, imp)
    work = jnp.where(visible, work, NEG)
    jbf = jb.astype(F32)
    sel = jnp.zeros(imp.shape, F32)
    for _ in range(min(N_SELECT, nsb)):
        mx = jnp.max(work, axis=1, keepdims=True)
        first = jnp.min(jnp.where(work == mx, jbf, float(nsb)), axis=1, keepdims=True)
        pick = jbf == first
        sel = jnp.where(pick, 1.0, sel)
        work = jnp.where(pick, 2.0 * NEG, work)
    return jnp.where(visible, sel, 0.0)


def _cmp_attend(q, kc, vc, d, slope):
    valid = d >= 0.0
    s = jnp.where(valid, _dot_nt(q, kc) * LN2 - slope * d, NEG)
    m = jnp.max(s, axis=1, keepdims=True)
    e = jnp.where(valid, jnp.exp(s - m), 0.0)
    p = e / jnp.maximum(jnp.sum(e, axis=1, keepdims=True), TINY)
    return p, _dot(p.astype(BF16), vc)


def _data_lanes(q, g):
    lane = lax.broadcasted_iota(jnp.int32, q.shape, 1)
    return jnp.where(lane // HEAD_DIM == g, q, jnp.zeros_like(q))


def _nsa_cmp_kernel(q_ref, kc_ref, vc_ref, ov_ref, oc_ref, unsel_ref, *, tq, nc_valid):
    qi = pl.program_id(1)
    ncp = kc_ref.shape[1]
    nsbp = ov_ref.shape[1]
    t = qi * tq + lax.broadcasted_iota(jnp.int32, (tq, 1), 0)
    n = lax.broadcasted_iota(jnp.int32, (1, ncp), 1)
    cend = jnp.where(n < nc_valid, n * CMP_STRIDE + CMP_BLOCK - 1, jnp.int32(2 ** 30))
    d = (t - cend).astype(F32)
    kc = kc_ref[0]
    vc = vc_ref[0]
    ov = ov_ref[...]
    for g in range(N_NSA_KV):
        psum = jnp.zeros((tq, ncp), F32)
        for j in range(NSA_GROUP):
            h = g * NSA_GROUP + j
            p, o = _cmp_attend(_data_lanes(q_ref[0, :, h * LANES:(h + 1) * LANES], g), kc, vc, d, _slope(h))
            oc_ref[0, :, h * LANES:(h + 1) * LANES] = o
            psum = psum + p
        unsel_ref[0, g] = (1.0 - _select_blocks(_dot_x2(psum, ov), t, nsbp)).astype(BF16)


def _nsa_cmp_prompt(nq_x, kc, vc, tq):
    b, s, _ = nq_x.shape
    ncp = kc.shape[1]
    nc_valid = (s - CMP_BLOCK) // CMP_STRIDE + 1
    nsb = -(-s // SEL_BLOCK)
    assert nsb <= LANES, "the block map rides in one 128-lane contraction tile"
    ov = _overlap_matrix(ncp, LANES)
    return pl.pallas_call(
        functools.partial(_nsa_cmp_kernel, tq=tq, nc_valid=nc_valid), grid=(b, s // tq),
        in_specs=[pl.BlockSpec((1, tq, 1024), lambda b, i: (b, i, 0)),
                  pl.BlockSpec((1, ncp, LANES), lambda b, i: (b, 0, 0)),
                  pl.BlockSpec((1, ncp, LANES), lambda b, i: (b, 0, 0)),
                  pl.BlockSpec((ncp, LANES), lambda b, i: (0, 0))],
        out_specs=[pl.BlockSpec((1, tq, 1024), lambda b, i: (b, i, 0)),
                   pl.BlockSpec((1, N_NSA_KV, tq, LANES), lambda b, i: (b, 0, i, 0))],
        out_shape=[jax.ShapeDtypeStruct((b, s, 1024), F32), jax.ShapeDtypeStruct((b, N_NSA_KV, s, LANES), BF16)],
        compiler_params=_cparams(("parallel", "parallel")), name="nsa_cmp_prompt",
    )(nq_x, kc, vc, ov)


def _pack_heads(o_a, o_b, g):
    lane = lax.broadcasted_iota(jnp.int32, o_a.shape, 1)
    first = jnp.where(lane < HEAD_DIM, o_a, pltpu.roll(o_b, HEAD_DIM, axis=1))
    second = jnp.where(lane < HEAD_DIM, pltpu.roll(o_a, HEAD_DIM, axis=1), o_b)
    return jnp.where(g == 0, first, second)


def _nsa_mix_kernel(qt_ref, kt_ref, q_ref, sk_ref, sv_ref, unsel_ref, wkp_ref, wvp_ref, wkc_ref, wvc_ref,
                    oc_ref, misc_ref, o_ref, m_ref, l_ref, acc_ref, *, t):
    g = pl.program_id(1)
    step = pl.program_id(2)
    qi = qt_ref[step]
    ki = kt_ref[step]

    @pl.when(ki == 0)
    def _():
        _init_softmax(m_ref, l_ref, acc_ref)

    def update(diag):
        k = sk_ref[0].astype(BF16)
        v = sv_ref[0].astype(BF16)
        lane = lax.broadcasted_iota(jnp.int32, (t, LANES), 1)
        row = lax.broadcasted_iota(jnp.int32, (t, LANES), 0)
        pos = ki * t + row
        spare = lane - HEAD_DIM * (1 - g)
        pos_lo = pos & 255
        pos_part = jnp.where((spare & 1) == 0, pos - pos_lo, pos_lo).astype(F32).astype(BF16)
        keys = jnp.where((spare >= 0) & (spare < 2 * N_BIAS_LANES), pos_part, k)
        blk = ki * (t // SEL_BLOCK) + row // SEL_BLOCK
        rhs = jnp.concatenate([keys, jnp.where(lane == blk, MASK_SCORE, 0.0).astype(BF16)], axis=1)
        unsel = unsel_ref[0, 0]
        for j in range(NSA_GROUP):
            lhs = jnp.concatenate([q_ref[0, :, j * LANES:(j + 1) * LANES], unsel], axis=1)
            s = _dot_nt(lhs, rhs)
            if diag:
                s = jnp.where(_lower_tri(t), s, NEG)
            _exp2_step(s, v, m_ref, l_ref, acc_ref, j)

    @pl.when(ki < qi)
    def _():
        update(False)

    @pl.when(ki == qi)
    def _():
        update(True)
        kw = jnp.concatenate([wkp_ref[0], wkc_ref[0]], axis=0).astype(BF16)
        vw = jnp.concatenate([wvp_ref[0], wvc_ref[0]], axis=0).astype(BF16)
        nw = 2 * WINDOW
        dw = WINDOW + lax.broadcasted_iota(jnp.int32, (WINDOW, nw), 0) - lax.broadcasted_iota(jnp.int32, (WINDOW, nw), 1)
        band = (dw >= 0) & (dw <= WINDOW)
        dwf = dw.astype(F32)
        gates = misc_ref[0]
        outs = []
        for j in range(NSA_GROUP):
            slope = jnp.where(g == 0, _slope(j), _slope(NSA_GROUP + j))
            q = _data_lanes(q_ref[0, :, j * LANES:(j + 1) * LANES], g)
            strips = []
            for r in range(t // WINDOW):
                kp = qi * t + (r - 1) * WINDOW + lax.broadcasted_iota(jnp.int32, (1, nw), 1)
                keys = slice(r * WINDOW, r * WINDOW + nw)
                s = _dot_nt(q[r * WINDOW:(r + 1) * WINDOW], kw[keys]) * LN2 - slope * dwf
                s = jnp.where(band & (kp >= 0), s, NEG)
                e = jnp.exp(s - jnp.max(s, axis=1, keepdims=True))
                strips.append(_dot(e.astype(BF16), vw[keys]) / jnp.sum(e, axis=1, keepdims=True))
            o_w = jnp.concatenate(strips, axis=0)
            c0 = N_FOX_HEADS + 3 * j
            c1 = c0 + 3 * NSA_GROUP
            gate = [jnp.where(g == 0, gates[:, c0 + c:c0 + c + 1], gates[:, c1 + c:c1 + c + 1]) for c in range(3)]
            outs.append(gate[0] * oc_ref[0, :, j * LANES:(j + 1) * LANES]
                        + gate[1] * _softmax_result(l_ref, acc_ref, j) + gate[2] * o_w)
        for c in range(NSA_GROUP // 2):
            o_ref[0, :, c * LANES:(c + 1) * LANES] = _pack_heads(outs[2 * c], outs[2 * c + 1], g).astype(BF16)


def _nsa_mix_prompt(nq_x, sk, sv, unsel, wk, wv, oc_x, misc, t):
    b, s, _ = nq_x.shape
    assert t % WINDOW == 0
    qt, kt = _tri_pairs(s // t)
    gw = NSA_GROUP * LANES
    qmap = lambda b, g, i, qt, kt: (b, qt[i], 0)
    gmap = lambda b, g, i, qt, kt: (b, qt[i], g)
    kmap = lambda b, g, i, qt, kt: (b, kt[i], 0)
    pmap = lambda b, g, i, qt, kt: (b, jnp.maximum(qt[i] * (t // WINDOW) - 1, 0), 0)
    grid_spec = pltpu.PrefetchScalarGridSpec(
        num_scalar_prefetch=2, grid=(b, N_NSA_KV, qt.shape[0]),
        in_specs=[
            pl.BlockSpec((1, t, gw), gmap),
            pl.BlockSpec((1, t, LANES), kmap), pl.BlockSpec((1, t, LANES), kmap),
            pl.BlockSpec((1, 1, t, LANES), lambda b, g, i, qt, kt: (b, g, qt[i], 0)),
            pl.BlockSpec((1, WINDOW, LANES), pmap), pl.BlockSpec((1, WINDOW, LANES), pmap),
            pl.BlockSpec((1, t, LANES), qmap), pl.BlockSpec((1, t, LANES), qmap),
            pl.BlockSpec((1, t, gw), gmap),
            pl.BlockSpec((1, t, LANES), qmap),
        ],
        out_specs=pl.BlockSpec((1, t, gw // 2), gmap),
        scratch_shapes=[pltpu.VMEM((NSA_GROUP, t, LANES), F32)] * 3,
    )
    return pl.pallas_call(
        functools.partial(_nsa_mix_kernel, t=t), grid_spec=grid_spec,
        out_shape=jax.ShapeDtypeStruct((b, s, 512), BF16),
        compiler_params=_cparams(("parallel", "parallel", "arbitrary")), name="nsa_mix_prompt",
    )(qt, kt, nq_x, sk, sv, unsel, wk, wv, wk, wv, oc_x, misc)


_ROUTE_E, _ROUTE_GATE, _ROUTE_RANK = 0, TOP_K, 2 * TOP_K


def _router_weights(w_router, b_router):
    d = w_router.shape[0]
    wp = jnp.concatenate([w_router, jnp.zeros((d, LANES - N_EXPERTS), F32)], axis=1)
    hi = wp.astype(BF16)
    lo = (wp - hi.astype(F32)).astype(BF16)
    bias = jnp.concatenate([b_router.astype(F32), jnp.full((LANES - N_EXPERTS,), NEG, F32)]).reshape(1, LANES)
    return hi, lo, bias


def _tail_kernel(x_ref, mf_ref, mn_ref, wo_ref, g_ref, wrh_ref, wrl_ref, br_ref, tri_ref,
                 x1_ref, h_ref, route_ref, cnt_ref):
    @pl.when(pl.program_id(0) == 0)
    def _():
        cnt_ref[...] = jnp.zeros(cnt_ref.shape, F32)

    half = mf_ref.shape[1]
    x1 = x_ref[...] + _dot(mf_ref[...], wo_ref[0:half, :]) + _dot(mn_ref[...], wo_ref[half:2 * half, :])
    x1_ref[...] = x1
    h = (x1 * lax.rsqrt(jnp.mean(x1 * x1, axis=-1, keepdims=True) + RMS_EPS)) * g_ref[...]
    h_hi, h_lo = _split2(h)
    bits = lax.bitcast_convert_type(h_hi.astype(F32), jnp.uint32)
    words = bits.shape[1] // 2
    h_ref[...] = bits[:, :words] | (bits[:, words:] >> 16)
    logits = _dot(h_hi, wrh_ref[...]) + _dot(h_lo, wrh_ref[...]) + _dot(h_hi, wrl_ref[...]) + br_ref[...]
    lane = lax.broadcasted_iota(jnp.int32, logits.shape, 1)
    lanef = lane.astype(F32)
    work = logits
    picks, vals, ids = [], [], []
    for _ in range(TOP_K):
        mx = jnp.max(work, axis=1, keepdims=True)
        first = jnp.min(jnp.where(work == mx, lanef, float(LANES)), axis=1, keepdims=True)
        pick = lanef == first
        work = jnp.where(pick, 2.0 * NEG, work)
        picks.append(pick); vals.append(mx); ids.append(first)
    ex = [jnp.exp(v - vals[0]) for v in vals]
    den = ex[0] + ex[1] + ex[2] + ex[3]
    onehot = jnp.zeros(logits.shape, F32)
    for p in picks:
        onehot = jnp.where(p, 1.0, onehot)
    carry = cnt_ref[0:1, :]
    ranks = _dot(tri_ref[...], onehot.astype(BF16)) + carry
    route = jnp.zeros(logits.shape, F32)
    for k in range(TOP_K):
        rank_k = jnp.sum(jnp.where(picks[k], ranks, 0.0), axis=1, keepdims=True)
        route = jnp.where(lane == _ROUTE_E + k, ids[k], route)
        route = jnp.where(lane == _ROUTE_GATE + k, ex[k] / den, route)
        route = jnp.where(lane == _ROUTE_RANK + k, rank_k, route)
    route_ref[...] = route
    cnt_ref[...] = jnp.broadcast_to(carry + jnp.sum(onehot, axis=0, keepdims=True), cnt_ref.shape)


def _tail(x2d, mf, mn, wo, g_ffn, wrh, wrl, br, tm):
    n, d = x2d.shape
    i = np.arange(tm)
    tri = jnp.asarray(i[:, None] > i[None, :], BF16)
    row = lambda c: pl.BlockSpec((tm, c), lambda i: (i, 0))
    full = lambda a: pl.BlockSpec(a.shape, lambda i: (0,) * a.ndim)
    g2 = g_ffn.reshape(1, d)
    return pl.pallas_call(
        _tail_kernel, grid=(n // tm,),
        in_specs=[row(d), row(mf.shape[1]), row(mn.shape[1]), full(wo), full(g2), full(wrh), full(wrl), full(br),
                  full(tri)],
        out_specs=[row(d), row(d // 2), row(LANES), pl.BlockSpec((8, LANES), lambda i: (0, 0))],
        out_shape=[jax.ShapeDtypeStruct((n, d), F32), jax.ShapeDtypeStruct((n, d // 2), jnp.uint32),
                   jax.ShapeDtypeStruct((n, LANES), F32), jax.ShapeDtypeStruct((8, LANES), F32)],
        compiler_params=_cparams(("arbitrary",)), name="tail",
    )(x2d, mf, mn, wo, g2, wrh, wrl, br, tri)


def _row_copy(src_hbm, src_row, dst, dst_row, sem):
    return pltpu.make_async_copy(src_hbm.at[pl.ds(src_row, 1)], dst.at[pl.ds(dst_row, 1)], sem)


def _dispatch_kernel(dest_ref, h_ref, xs_in_hbm, xs_hbm, sem, *, tm):
    del xs_in_hbm

    def issue(r, c):
        for k in range(TOP_K):
            _row_copy(h_ref, r, xs_hbm, dest_ref[0, 0, TOP_K * r + k], sem).start()
        return c

    def drain(r, c):
        for k in range(TOP_K):
            _row_copy(h_ref, r, xs_hbm, dest_ref[0, 0, TOP_K * r + k], sem).wait()
        return c

    lax.fori_loop(0, tm, issue, 0)
    lax.fori_loop(0, tm, drain, 0)


def _dispatch(h, dest, xs, tm):
    n, d = h.shape
    dest3 = dest.reshape(n // tm, 1, tm * TOP_K)
    return pl.pallas_call(
        functools.partial(_dispatch_kernel, tm=tm), grid=(n // tm,),
        in_specs=[pl.BlockSpec((1, 1, tm * TOP_K), lambda i: (i, 0, 0), memory_space=pltpu.SMEM),
                  pl.BlockSpec((tm, d), lambda i: (i, 0)), pl.BlockSpec(memory_space=pl.ANY)],
        out_specs=pl.BlockSpec(memory_space=pl.ANY),
        out_shape=jax.ShapeDtypeStruct(xs.shape, xs.dtype),
        scratch_shapes=[pltpu.SemaphoreType.DMA(())],
        input_output_aliases={2: 0},
        compiler_params=_cparams(("arbitrary",)), name="moe_dispatch",
    )(dest3, h, xs)


_PAIR = 2 * LANES


def _split_gate_up_kernel(w_ref, p_ref, g_ref, u_ref):
    p = p_ref[...]
    for c in range(w_ref.shape[2] // _PAIR):
        t = _dot(w_ref[0, :, c * _PAIR:(c + 1) * _PAIR].astype(BF16), p)
        g_ref[0, :, c * LANES:(c + 1) * LANES] = t[:, :LANES].astype(BF16)
        u_ref[0, :, c * LANES:(c + 1) * LANES] = t[:, LANES:].astype(BF16)


def _split_gate_up(w_gate_up, rows):
    e, d, de2 = w_gate_up.shape
    perm = np.zeros((_PAIR, _PAIR), np.float32)
    j = np.arange(LANES)
    perm[2 * j, j] = 1.0
    perm[2 * j + 1, LANES + j] = 1.0
    ospec = pl.BlockSpec((1, rows, de2 // 2), lambda i, r: (i, r, 0))
    return pl.pallas_call(
        _split_gate_up_kernel, grid=(e, d // rows),
        in_specs=[pl.BlockSpec((1, rows, de2), lambda i, r: (i, r, 0)),
                  pl.BlockSpec((_PAIR, _PAIR), lambda i, r: (0, 0))],
        out_specs=[ospec, ospec], out_shape=[jax.ShapeDtypeStruct((e, d, de2 // 2), BF16)] * 2,
        compiler_params=_cparams(("parallel", "parallel")), name="split_gate_up",
    )(w_gate_up, jnp.asarray(perm, BF16))


def _expert_kernel(be_ref, nused_ref, x_ref, wg_ref, wu_ref, bg_ref, bu_ref, wd_ref, bd_ref, y_ref):
    @pl.when(pl.program_id(0) < nused_ref[0])
    def _():
        words = x_ref[...]
        left = lax.bitcast_convert_type(words & jnp.uint32(0xFFFF0000), F32)
        right = lax.bitcast_convert_type(words << 16, F32)
        x = jnp.concatenate([left, right], axis=1).astype(BF16)
        g = jnp.minimum(_dot(x, wg_ref[0]) + bg_ref[0], SWIGLU_LIMIT)
        u = jnp.clip(_dot(x, wu_ref[0]) + bu_ref[0], -SWIGLU_LIMIT, SWIGLU_LIMIT)
        act = (u + 1.0) * g * jax.nn.sigmoid(SWIGLU_ALPHA * g)
        y_ref[...] = _dot(act.astype(BF16), wd_ref[0]) + bd_ref[0]

    @pl.when(pl.program_id(0) >= nused_ref[0])
    def _():
        y_ref[...] = jnp.zeros(y_ref.shape, F32)


def _experts(xs, blk_expert, nused, wg, wu, bg, bu, wd, bd, mb):
    n_slots, words = xs.shape
    _, d, de = wg.shape
    rows = lambda i, be, nu: (jnp.minimum(i, nu[0] - 1), 0)
    wmap = lambda i, be, nu: (be[i], 0, 0)
    grid_spec = pltpu.PrefetchScalarGridSpec(
        num_scalar_prefetch=2, grid=(n_slots // mb,),
        in_specs=[pl.BlockSpec((mb, words), rows),
                  pl.BlockSpec((1, d, de), wmap), pl.BlockSpec((1, d, de), wmap),
                  pl.BlockSpec((1, 1, de), wmap), pl.BlockSpec((1, 1, de), wmap),
                  pl.BlockSpec((1, de, d), wmap), pl.BlockSpec((1, 1, d), wmap)],
        out_specs=pl.BlockSpec((mb, d), lambda i, be, nu: (i, 0)),
    )
    return pl.pallas_call(
        _expert_kernel, grid_spec=grid_spec, out_shape=jax.ShapeDtypeStruct((n_slots, d), F32),
        compiler_params=_cparams(("arbitrary",)), name="moe_experts",
    )(blk_expert, nused, xs, wg, wu, bg, bu, wd, bd)


def _combine_kernel(dest_ref, route_ref, x1_ref, ys_hbm, o_ref, buf_ref, sem, *, tm):
    def issue(r, c):
        for k in range(TOP_K):
            _row_copy(ys_hbm, dest_ref[0, 0, TOP_K * r + k], buf_ref.at[k], r, sem).start()
        return c

    def drain(r, c):
        for k in range(TOP_K):
            _row_copy(ys_hbm, dest_ref[0, 0, TOP_K * r + k], buf_ref.at[k], r, sem).wait()
        return c

    lax.fori_loop(0, tm, issue, 0)
    lax.fori_loop(0, tm, drain, 0)
    route = route_ref[...]
    y = route[:, _ROUTE_GATE:_ROUTE_GATE + 1] * buf_ref[0]
    for k in range(1, TOP_K):
        y = y + route[:, _ROUTE_GATE + k:_ROUTE_GATE + k + 1] * buf_ref[k]
    o_ref[...] = x1_ref[...] + y


def _combine(x1, route, dest, ys, tm):
    n, d = x1.shape
    dest3 = dest.reshape(n // tm, 1, tm * TOP_K)
    row = lambda c: pl.BlockSpec((tm, c), lambda i: (i, 0))
    return pl.pallas_call(
        functools.partial(_combine_kernel, tm=tm), grid=(n // tm,),
        in_specs=[pl.BlockSpec((1, 1, tm * TOP_K), lambda i: (i, 0, 0), memory_space=pltpu.SMEM),
                  row(LANES), row(d), pl.BlockSpec(memory_space=pl.ANY)],
        out_specs=row(d), out_shape=jax.ShapeDtypeStruct((n, d), F32),
        scratch_shapes=[pltpu.VMEM((TOP_K, tm, d), F32), pltpu.SemaphoreType.DMA(())],
        compiler_params=_cparams(("arbitrary",)), name="moe_combine",
    )(dest3, route, x1, ys)


def _moe_plan(route_p, cnt_p, route_s, cnt_s, mb):
    n_tok = route_p.shape[0] + route_s.shape[0]
    cp = cnt_p[0, :N_EXPERTS].astype(jnp.int32)
    cs = cnt_s[0, :N_EXPERTS].astype(jnp.int32)
    counts = cp + cs
    padded = (counts + mb - 1) // mb * mb
    pend = jnp.cumsum(padded)
    pstart = pend - padded
    nblk = -(-n_tok * TOP_K // mb) + N_EXPERTS
    first_slot = jnp.arange(nblk, dtype=jnp.int32) * mb
    blk_expert = jnp.clip(jnp.sum(pend[None, :] <= first_slot[:, None], axis=1), 0, N_EXPERTS - 1).astype(jnp.int32)
    nused = (pend[-1:] // mb).astype(jnp.int32)

    def dest(route, first):
        e = route[:, _ROUTE_E:_ROUTE_E + TOP_K].astype(jnp.int32)
        rank = route[:, _ROUTE_RANK:_ROUTE_RANK + TOP_K].astype(jnp.int32)
        onehot = e[..., None] == jnp.arange(N_EXPERTS, dtype=jnp.int32)
        return jnp.sum(jnp.where(onehot, first, 0), axis=-1) + rank

    return dest(route_p, pstart), dest(route_s, pstart + cp), blk_expert, nused, nblk


def _page_spec(shape, n_pages, pp, j):
    return pl.BlockSpec((1,) + shape, lambda b, c, pt: (pt[b * n_pages + c * pp + j],) + (0,) * len(shape))


def _slope_col():
    h = lax.broadcasted_iota(jnp.int32, (N_NSA_HEADS, 1), 0).astype(F32)
    return jnp.exp2(-8.0 * (h + 1.0) / N_NSA_HEADS)


def _rounded(ref):
    return ref[0].astype(BF16).astype(F32)


def _append_key(q, s_bias, k_new, v_new, m_ref, l_ref, acc_ref):
    s = jnp.sum(q.astype(F32) * k_new, axis=1, keepdims=True) * LN2 + s_bias
    m_prev = m_ref[0]
    m_new = jnp.maximum(m_prev, s)
    alpha = jnp.exp(m_prev - m_new)
    p = jnp.exp(s - m_new)
    l_ref[0] = alpha * l_ref[0] + p
    acc_ref[0] = alpha * acc_ref[0] + p * v_new
    m_ref[0] = m_new


def _fox_sample_kernel(pt_ref, qt_ref, qh_ref, *refs, pp):
    del pt_ref
    k_refs, v_refs = refs[:pp], refs[pp:2 * pp]
    c_ref, kn_ref, vn_ref, cn_ref, o_ref, m_ref, l_ref, acc_ref = refs[2 * pp:]
    c = pl.program_id(1)
    rows = PAGE_SIZE * N_FOX_HEADS

    @pl.when(c == 0)
    def _():
        _init_softmax(m_ref, l_ref, acc_ref)

    qt = qt_ref[0]
    ones = jnp.ones((LANES, HEAD_DIM), BF16)
    own = (lax.broadcasted_iota(jnp.int32, (N_FOX_HEADS, LANES), 1)
           == lax.broadcasted_iota(jnp.int32, (N_FOX_HEADS, LANES), 0))
    diag = lambda x: jnp.sum(jnp.where(own, x, 0.0), axis=1, keepdims=True)

    def update(s, weigh):
        lead = s.ndim - 2
        m_prev = m_ref[...]
        m_new = jnp.maximum(m_prev, jnp.max(s, axis=0) if lead else s)
        alpha = jnp.exp2(m_prev - m_new)
        p = jnp.where(own, jnp.exp2(s - m_new), 0.0)
        l_ref[...] = alpha * l_ref[...] + (jnp.sum(p, axis=0) if lead else p)
        acc_ref[...] = diag(alpha) * acc_ref[...] + weigh(p)
        m_ref[...] = m_new

    for j in range(pp):
        kp = k_refs[j][0].reshape(rows, HEAD_DIM).astype(BF16)
        s = _dot(kp, qt).reshape(PAGE_SIZE, N_FOX_HEADS, LANES)
        s = s - c_ref[0, j * PAGE_SIZE:(j + 1) * PAGE_SIZE, :][:, None, :]

        def weigh(p, j=j):
            spread = _dot(p.reshape(rows, LANES).astype(BF16), ones)
            return jnp.sum(spread.reshape(PAGE_SIZE, N_FOX_HEADS, HEAD_DIM) * v_refs[j][0], axis=0)

        update(s, weigh)

    @pl.when(c == pl.num_programs(1) - 1)
    def _():
        s_new = jnp.sum(qh_ref[0].astype(F32) * _rounded(kn_ref), axis=1, keepdims=True)
        update(jnp.where(own, s_new - cn_ref[0], NEG), lambda p: diag(p) * vn_ref[0])
        o_ref[0] = acc_ref[...] / diag(l_ref[...])


def _fox_sample(pt, qt, qh, cache_k, cache_v, cs, k_new, v_new, c_new, pp):
    b, n_pages = pt.shape
    bmap = lambda b, c, pt: (b, 0, 0)
    head = pl.BlockSpec((1, N_FOX_HEADS, HEAD_DIM), bmap)
    grid_spec = pltpu.PrefetchScalarGridSpec(
        num_scalar_prefetch=1, grid=(b, n_pages // pp),
        in_specs=([pl.BlockSpec((1, HEAD_DIM, LANES), bmap), head]
                  + [_page_spec((PAGE_SIZE, N_FOX_HEADS, HEAD_DIM), n_pages, pp, j) for j in range(pp)] * 2
                  + [pl.BlockSpec((1, pp * PAGE_SIZE, LANES), lambda b, c, pt: (b, c, 0)),
                     head, head, pl.BlockSpec((1, N_FOX_HEADS, LANES), bmap)]),
        out_specs=head,
        scratch_shapes=[pltpu.VMEM((N_FOX_HEADS, LANES), F32), pltpu.VMEM((N_FOX_HEADS, LANES), F32),
                        pltpu.VMEM((N_FOX_HEADS, HEAD_DIM), F32)],
    )
    return pl.pallas_call(
        functools.partial(_fox_sample_kernel, pp=pp), grid_spec=grid_spec,
        out_shape=jax.ShapeDtypeStruct((b, N_FOX_HEADS, HEAD_DIM), F32),
        compiler_params=_cparams(("parallel", "arbitrary")), name="fox_sample",
    )(pt.reshape(-1), qt, qh, *([cache_k] * pp), *([cache_v] * pp), cs, k_new, v_new, c_new)


def _cmp_sample_kernel(pt_ref, q_ref, *refs, pp, past):
    del pt_ref
    xk_refs, xv_refs = refs[:pp], refs[pp:2 * pp]
    w_ref, pe_ref, g_ref, hm_ref, ov_ref, oc_ref, sel_ref, a_ref = refs[2 * pp:]
    c = pl.program_id(1)
    rpp = PAGE_SIZE // CMP_STRIDE
    xk = jnp.concatenate([r[0] for r in xk_refs], axis=0)
    xv = jnp.concatenate([r[0] for r in xv_refs], axis=0)
    rows = pl.ds(pl.multiple_of(c * (rpp * pp), rpp * pp), rpp * pp)
    for i, x in ((0, xk), (1, xk), (2, xv), (3, xv)):
        a_ref[i, rows, :] = _dot((x + pe_ref[i:i + 1, :]).astype(BF16), w_ref[i])

    @pl.when(c == pl.num_programs(1) - 1)
    def _():
        n = a_ref.shape[1]
        nsbp = ov_ref.shape[1]
        kc = _head_rms(a_ref[0] + pltpu.roll(a_ref[1], n - 1, axis=0), g_ref[...], hm_ref[...]).astype(BF16)
        vc = (a_ref[2] + pltpu.roll(a_ref[3], n - 1, axis=0)).astype(BF16)
        nc_valid = (past + 1 - CMP_BLOCK) // CMP_STRIDE + 1
        idx = lax.broadcasted_iota(jnp.int32, (N_NSA_HEADS, n), 1)
        cend = jnp.where(idx < nc_valid, idx * CMP_STRIDE + CMP_BLOCK - 1, jnp.int32(2 ** 30))
        p, o = _cmp_attend(q_ref[0], kc, vc, (past - cend).astype(F32), _slope_col())
        oc_ref[0] = o
        psum = jnp.concatenate([jnp.sum(p[0:NSA_GROUP], axis=0, keepdims=True),
                                jnp.sum(p[NSA_GROUP:2 * NSA_GROUP], axis=0, keepdims=True),
                                jnp.zeros((N_NSA_HEADS - N_NSA_KV, n), F32)], axis=0)
        t = jnp.full((N_NSA_HEADS, 1), past, jnp.int32)
        sel = _select_blocks(_dot_x2(psum, ov_ref[...]), t, nsbp)
        sel_ref[0] = jnp.concatenate([jnp.broadcast_to(sel[0:1], (NSA_GROUP, nsbp)),
                                      jnp.broadcast_to(sel[1:2], (NSA_GROUP, nsbp))], axis=0).astype(BF16)


def _cmp_sample(pt, qg, cache_ck, cache_cv, w4, pe4, g_k_cmp2, hm, pp):
    b, n_pages = pt.shape
    past = n_pages * PAGE_SIZE
    wide = cache_ck.shape[2]
    rpp = PAGE_SIZE // CMP_STRIDE
    n = n_pages * rpp
    nsb = -(-(past + 1) // SEL_BLOCK)
    nsbp = -(-nsb // LANES) * LANES
    ov = _overlap_matrix(n, nsbp)
    bmap = lambda b, c, pt: (b, 0, 0)
    full = lambda a: pl.BlockSpec(a.shape, lambda b, c, pt: (0,) * a.ndim)
    grid_spec = pltpu.PrefetchScalarGridSpec(
        num_scalar_prefetch=1, grid=(b, n_pages // pp),
        in_specs=([pl.BlockSpec((1, N_NSA_HEADS, LANES), bmap)]
                  + [_page_spec((rpp, wide), n_pages, pp, j) for j in range(pp)] * 2
                  + [full(w4), full(pe4), full(g_k_cmp2), full(hm), full(ov)]),
        out_specs=[pl.BlockSpec((1, N_NSA_HEADS, LANES), bmap), pl.BlockSpec((1, N_NSA_HEADS, nsbp), bmap)],
        scratch_shapes=[pltpu.VMEM((4, n, LANES), F32)],
    )
    return pl.pallas_call(
        functools.partial(_cmp_sample_kernel, pp=pp, past=past), grid_spec=grid_spec,
        out_shape=[jax.ShapeDtypeStruct((b, N_NSA_HEADS, LANES), F32),
                   jax.ShapeDtypeStruct((b, N_NSA_HEADS, nsbp), BF16)],
        compiler_params=_cparams(("parallel", "arbitrary")), name="cmp_sample",
    )(pt.reshape(-1), qg, *([cache_ck] * pp), *([cache_cv] * pp), w4, pe4, g_k_cmp2, hm, ov)


def _mix_sample_kernel(pt_ref, q_ref, sel_ref, *refs, pp, past):
    del pt_ref
    k_refs, v_refs = refs[:pp], refs[pp:2 * pp]
    (skn_ref, svn_ref, stk_ref, stv_ref, wkn_ref, wvn_ref, oc_ref, gate_ref,
     o_ref, m_ref, l_ref, acc_ref) = refs[2 * pp:]
    c = pl.program_id(1)
    tc = pp * PAGE_SIZE
    nsbp = sel_ref.shape[2]

    @pl.when(c == 0)
    def _():
        _init_softmax(m_ref, l_ref, acc_ref)

    q = q_ref[0]
    slope = _slope_col()
    k = jnp.concatenate([r[0] for r in k_refs], axis=0).astype(BF16)
    v = jnp.concatenate([r[0] for r in v_refs], axis=0).astype(BF16)
    col = lax.broadcasted_iota(jnp.int32, (nsbp, tc), 1)
    blk = lax.broadcasted_iota(jnp.int32, (nsbp, tc), 0)
    expand = (blk == c * (tc // SEL_BLOCK) + col // SEL_BLOCK).astype(BF16)
    chosen = _dot(sel_ref[0], expand) > 0.5
    kpos = (c * tc - past + lax.broadcasted_iota(jnp.int32, (1, tc), 1)).astype(F32)
    s = _dot_nt(q, k) * LN2 + slope * kpos
    _softmax_step(jnp.where(chosen, s, NEG), v, m_ref, l_ref, acc_ref, 0)

    @pl.when(c == pl.num_programs(1) - 1)
    def _():
        _append_key(q, 0.0, _rounded(skn_ref), _rounded(svn_ref), m_ref, l_ref, acc_ref)
        o_s = acc_ref[0] / l_ref[0]
        nbuf = stk_ref.shape[1]
        dwin = (nbuf - lax.broadcasted_iota(jnp.int32, (1, nbuf), 1)).astype(F32)
        s_w = _dot_nt(q, stk_ref[0].astype(BF16)) * LN2 - slope * dwin
        s_n = jnp.sum(q.astype(F32) * _rounded(wkn_ref), axis=1, keepdims=True) * LN2
        mw = jnp.maximum(jnp.max(s_w, axis=1, keepdims=True), s_n)
        e = jnp.exp(s_w - mw)
        e_n = jnp.exp(s_n - mw)
        o_w = ((_dot(e.astype(BF16), stv_ref[0].astype(BF16)) + e_n * _rounded(wvn_ref))
               / (jnp.sum(e, axis=1, keepdims=True) + e_n))
        g = gate_ref[0]
        o_ref[0] = g[:, 0:1] * oc_ref[0] + g[:, 1:2] * o_s + g[:, 2:3] * o_w


def _mix_sample(pt, qg, selh, cache_sk, cache_sv, sk_new, sv_new, st_k, st_v, wk_new, wv_new, oc, gates, pp):
    b, n_pages = pt.shape
    past = n_pages * PAGE_SIZE
    nsbp = selh.shape[2]
    nbuf = st_k.shape[1]
    bmap = lambda b, c, pt: (b, 0, 0)
    one = pl.BlockSpec((1, 1, LANES), bmap)
    grid_spec = pltpu.PrefetchScalarGridSpec(
        num_scalar_prefetch=1, grid=(b, n_pages // pp),
        in_specs=([pl.BlockSpec((1, N_NSA_HEADS, LANES), bmap), pl.BlockSpec((1, N_NSA_HEADS, nsbp), bmap)]
                  + [_page_spec((PAGE_SIZE, LANES), n_pages, pp, j) for j in range(pp)] * 2
                  + [one, one, pl.BlockSpec((1, nbuf, LANES), bmap), pl.BlockSpec((1, nbuf, LANES), bmap), one, one,
                     pl.BlockSpec((1, N_NSA_HEADS, LANES), bmap), pl.BlockSpec((1, N_NSA_HEADS, 3), bmap)]),
        out_specs=pl.BlockSpec((1, N_NSA_HEADS, LANES), bmap),
        scratch_shapes=[pltpu.VMEM((1, N_NSA_HEADS, 1), F32), pltpu.VMEM((1, N_NSA_HEADS, 1), F32),
                        pltpu.VMEM((1, N_NSA_HEADS, LANES), F32)],
    )
    return pl.pallas_call(
        functools.partial(_mix_sample_kernel, pp=pp, past=past), grid_spec=grid_spec,
        out_shape=jax.ShapeDtypeStruct((b, N_NSA_HEADS, LANES), F32),
        compiler_params=_cparams(("parallel", "arbitrary")), name="mix_sample",
    )(pt.reshape(-1), qg, selh, *([cache_sk] * pp), *([cache_sv] * pp), sk_new, sv_new, st_k, st_v,
      wk_new, wv_new, oc, gates)


PROMPT_ROWS = 256
ATTN_TILE = 1024
CMP_Q_TILE = 1024
MOE_ROWS = 256
WEIGHT_ROWS = 512
PAGES_PER_STEP = 8


def _layer(xp, xs, c_fox_k, c_fox_v, c_fox_logf, c_cmp_k, c_cmp_v, c_sel_k, c_sel_v, st_win_k, st_win_v, page_table,
           g_attn, w_in, b_fox_f, b_nsa_gate, g_q_fox, g_k_fox, g_q_nsa, g_k_sel, g_k_win, g_k_cmp,
           pe_cmp_k, pe_cmp_v, w_cmp_k, w_cmp_v, w_out, g_ffn, w_router, b_router, w_gate_up, b_gate_up,
           w_down, b_down):
    bp, sp, d = xp.shape
    bs = xs.shape[0]
    n_pool = c_fox_k.shape[0]
    n_pages = page_table.shape[1]
    past = n_pages * PAGE_SIZE

    w, bias = _proj_weights(w_in, b_fox_f, b_nsa_gate)
    tile2 = lambda g: jnp.tile(g, 2)
    gains = jnp.stack([tile2(g_q_fox), tile2(g_q_nsa), tile2(g_k_fox), tile2(g_k_sel), tile2(g_k_win)])
    hm = _head_mean_matrix()
    wk2, pek = _cmp_weights(w_cmp_k, pe_cmp_k)
    wv2, pev = _cmp_weights(w_cmp_v, pe_cmp_v)
    w4 = jnp.concatenate([wk2, wv2])
    pe4 = jnp.concatenate([pek, pev])
    g_cmp2 = tile2(g_k_cmp).reshape(1, LANES)
    wo = w_out.astype(BF16)
    wrh, wrl, br = _router_weights(w_router, b_router)
    wg, wu = _split_gate_up(w_gate_up, WEIGHT_ROWS)
    bg = b_gate_up[:, None, 0::2]
    bu = b_gate_up[:, None, 1::2]
    wd = w_down.astype(BF16)
    bd = b_down[:, None, :]

    pr = _project(xp.reshape(bp * sp, d), g_attn, w, bias, gains, hm, PROMPT_ROWS)
    fq, nq, fk, fv, ck, cv, sk, sv, wk, wv, misc = [a.reshape(bp, sp, -1) for a in pr]
    logf = misc[:, :, :N_FOX_HEADS]
    mix_fox = _fox_prompt(fq, fk, fv, _fox_bias(misc), ATTN_TILE)
    kc, vc = _cmp_prompt(ck, cv, w4, pe4, g_cmp2, hm)
    oc, unsel = _nsa_cmp_prompt(nq, kc, vc, CMP_Q_TILE)
    mix_nsa = _nsa_mix_prompt(nq, sk, sv, unsel, wk, wv, oc, misc, ATTN_TILE)
    x1p, hp, route_p, cnt_p = _tail(xp.reshape(bp * sp, d), mix_fox.reshape(bp * sp, -1),
                                    mix_nsa.reshape(bp * sp, -1), wo, g_ffn, wrh, wrl, br, PROMPT_ROWS)

    sr = _project(xs.reshape(bs, d), g_attn, w, bias, gains, hm, bs)
    fq_s, nq_s, fk_s, fv_s, ck_s, cv_s, sk_s, sv_s, wk_s, wv_s, misc_s = sr
    logf_s = misc_s[:, :N_FOX_HEADS]
    gates_s = misc_s[:, N_FOX_HEADS:N_FOX_HEADS + 3 * N_NSA_HEADS].reshape(bs, N_NSA_HEADS, 3)
    q_heads = fq_s.reshape(bs, N_FOX_HEADS, 2, HEAD_DIM)
    q_heads = jnp.stack([q_heads[:, h, h % 2] for h in range(N_FOX_HEADS)], axis=1)
    qt = jnp.pad(jnp.transpose(q_heads, (0, 2, 1)), ((0, 0), (0, 0), (0, LANES - N_FOX_HEADS)))
    qg = nq_s.reshape(bs, N_NSA_HEADS, N_NSA_KV, HEAD_DIM)
    in_group = (jnp.arange(N_NSA_HEADS)[:, None] // NSA_GROUP == jnp.arange(N_NSA_KV)[None, :])[None, :, :, None]
    qg = jnp.where(in_group, qg, jnp.zeros_like(qg)).reshape(bs, N_NSA_HEADS, LANES)
    logf_past = c_fox_logf[page_table].reshape(bs, past, N_FOX_HEADS).astype(F32)
    ct_s = _cumsum_lanes(jnp.transpose(logf_past, (0, 2, 1)))
    c_new = jnp.broadcast_to(((ct_s[:, :, -1] + logf_s) * LOG2E)[:, :, None], (bs, N_FOX_HEADS, LANES))
    cs = jnp.pad(jnp.transpose(ct_s * LOG2E, (0, 2, 1)), ((0, 0), (0, 0), (0, LANES - N_FOX_HEADS)))
    row3 = lambda a: a.reshape(bs, 1, -1)
    per_head = lambda a: a.reshape(bs, N_FOX_HEADS, HEAD_DIM)
    o_fox_s = _fox_sample(page_table, qt, q_heads, c_fox_k, c_fox_v, cs, per_head(fk_s), per_head(fv_s), c_new,
                          PAGES_PER_STEP)
    wide = CMP_STRIDE * LANES
    oc_s, selh = _cmp_sample(page_table, qg, c_cmp_k.reshape(n_pool, -1, wide), c_cmp_v.reshape(n_pool, -1, wide),
                             w4, pe4, g_cmp2, hm, PAGES_PER_STEP)
    nbuf = st_win_k.shape[1]
    o_nsa_s = _mix_sample(page_table, qg, selh, c_sel_k.reshape(n_pool, PAGE_SIZE, -1),
                          c_sel_v.reshape(n_pool, PAGE_SIZE, -1), row3(sk_s), row3(sv_s),
                          st_win_k.reshape(bs, nbuf, -1), st_win_v.reshape(bs, nbuf, -1), row3(wk_s), row3(wv_s),
                          oc_s, gates_s, PAGES_PER_STEP)
    mf_s = o_fox_s.reshape(bs, -1).astype(BF16)
    o_nsa_s = o_nsa_s.reshape(bs, N_NSA_HEADS, N_NSA_KV, HEAD_DIM)
    mn_s = jnp.stack([o_nsa_s[:, h, h // NSA_GROUP] for h in range(N_NSA_HEADS)], axis=1).reshape(bs, -1).astype(BF16)
    x1s, hs, route_s, cnt_s = _tail(xs.reshape(bs, d), mf_s, mn_s, wo, g_ffn, wrh, wrl, br, bs)

    dest_p, dest_s, blk_expert, nused, nblk = _moe_plan(route_p, cnt_p, route_s, cnt_s, MOE_ROWS)
    slots = jnp.zeros((nblk * MOE_ROWS, d // 2), jnp.uint32)
    slots = _dispatch(hp, dest_p, slots, PROMPT_ROWS)
    slots = _dispatch(hs, dest_s, slots, bs)
    ys = _experts(slots, blk_expert, nused, wg, wu, bg, bu, wd, bd, MOE_ROWS)
    yp = _combine(x1p, route_p, dest_p, ys, PROMPT_ROWS).reshape(bp, sp, d)
    y_s = _combine(x1s, route_s, dest_s, ys, bs).reshape(bs, 1, d)

    heads = lambda a, b_, s_: a.reshape(b_, s_, -1, HEAD_DIM)
    wp = min(WINDOW, sp)
    rows_p = (heads(fk, bp, sp), heads(fv, bp, sp), logf, heads(ck, bp, sp), heads(cv, bp, sp), heads(sk, bp, sp),
              heads(sv, bp, sp), heads(wk, bp, sp)[:, -wp:], heads(wv, bp, sp)[:, -wp:])
    win_k_s = jnp.concatenate([st_win_k, heads(wk_s, bs, 1)], axis=1)[:, -nbuf:]
    win_v_s = jnp.concatenate([st_win_v, heads(wv_s, bs, 1)], axis=1)[:, -nbuf:]
    rows_s = (heads(fk_s, bs, 1), heads(fv_s, bs, 1), logf_s.reshape(bs, 1, -1), heads(ck_s, bs, 1),
              heads(cv_s, bs, 1), heads(sk_s, bs, 1), heads(sv_s, bs, 1), win_k_s, win_v_s)
    return yp, y_s, rows_p, rows_s


def kernel(x_prompt, x_sample, cache_fox_k, cache_fox_v, cache_fox_logf, cache_cmp_k, cache_cmp_v, cache_sel_k,
           cache_sel_v, state_win_k, state_win_v, page_table, g_attn, w_in, b_fox_f, b_nsa_gate, g_q_fox, g_k_fox,
           g_q_nsa, g_k_sel, g_k_win, g_k_cmp, pe_cmp_k, pe_cmp_v, w_cmp_k, w_cmp_v, w_out, g_ffn, w_router,
           b_router, w_gate_up, b_gate_up, w_down, b_down):
    depth = w_in.shape[0]
    assert depth == 1, "the two groups are threaded through a single layer"
    l = 0
    yp, y_s, rows_p, rows_s = _layer(
        x_prompt, x_sample, cache_fox_k[l], cache_fox_v[l], cache_fox_logf[l], cache_cmp_k[l], cache_cmp_v[l],
        cache_sel_k[l], cache_sel_v[l], state_win_k[l], state_win_v[l], page_table,
        g_attn[l], w_in[l], b_fox_f[l], b_nsa_gate[l], g_q_fox[l], g_k_fox[l], g_q_nsa[l], g_k_sel[l], g_k_win[l],
        g_k_cmp[l], pe_cmp_k[l], pe_cmp_v[l], w_cmp_k[l], w_cmp_v[l], w_out[l], g_ffn[l], w_router[l], b_router[l],
        w_gate_up[l], b_gate_up[l], w_down[l], b_down[l])
    return (yp, y_s) + tuple(r[None] for r in rows_p) + tuple(r[None] for r in rows_s)
```

```python
import functools

import numpy as np
import jax
import jax.numpy as jnp
from jax import lax
from jax.experimental import pallas as pl
from jax.experimental.pallas import tpu as pltpu

F32 = jnp.float32
BF16 = jnp.bfloat16

HEAD_DIM = 64
LANES = 128
N_FOX_HEADS = 8
N_NSA_HEADS = 8
N_NSA_KV = 2
NSA_GROUP = N_NSA_HEADS // N_NSA_KV
CMP_BLOCK = 32
CMP_STRIDE = 16
SEL_BLOCK = 64
N_SELECT = 16
WINDOW = 512
FORCE_SCORE = 1.0e4
N_EXPERTS = 32
TOP_K = 4
SWIGLU_LIMIT = 7.0
SWIGLU_ALPHA = 1.702
RMS_EPS = 1e-6
PAGE_SIZE = 128
NEG = -1.0e30
LOG2E = float(np.log2(np.e))
LN2 = float(np.log(2.0))
N_BIAS_LANES = 3
MASK_SCORE = -float(2 ** 17)
TINY = float(np.finfo(np.float32).tiny)
VMEM_LIMIT = 56 * 1024 * 1024


def _cparams(sem):
    return pltpu.CompilerParams(dimension_semantics=sem, vmem_limit_bytes=VMEM_LIMIT)


def _dot(a, b):
    return jnp.dot(a, b, preferred_element_type=F32)


def _dot_nt(a, b):
    return lax.dot_general(a, b, (((1,), (1,)), ((), ())), preferred_element_type=F32)


def _dot_tn(a, b):
    return lax.dot_general(a, b, (((0,), (0,)), ((), ())), preferred_element_type=F32)


def _split2(x):
    hi = x.astype(BF16)
    lo = (x - hi.astype(F32)).astype(BF16)
    return hi, lo


def _split3(x):
    hi = x.astype(BF16)
    r = x - hi.astype(F32)
    mid = r.astype(BF16)
    lo = (r - mid.astype(F32)).astype(BF16)
    return hi, mid, lo


def _dot_x2(x, m):
    hi, lo = _split2(x)
    return _dot(hi, m) + _dot(lo, m)


def _head_rms(t, gain, hm):
    outs = []
    for c in range(t.shape[1] // LANES):
        tc = t[:, c * LANES:(c + 1) * LANES]
        ms = _dot_x2(tc * tc, hm)
        outs.append(tc * lax.rsqrt(ms + RMS_EPS) * gain)
    return outs[0] if len(outs) == 1 else jnp.concatenate(outs, axis=1)


def _head_mean_matrix():
    i = np.arange(LANES)
    return jnp.asarray((i[:, None] // HEAD_DIM == i[None, :] // HEAD_DIM) / HEAD_DIM, BF16)


_FQ0, _NQ0, _FK0, _FV0, _KV0, _MISC0, _PROJ_COLS = 0, 1024, 2048, 2560, 3072, 3840, 3968


def _proj_weights(w_in, b_fox_f, b_nsa_gate):
    fw = N_FOX_HEADS * HEAD_DIM
    kvw = N_NSA_KV * HEAD_DIM
    o = 0
    fq = w_in[:, o:o + fw]; o += fw
    fk = w_in[:, o:o + fw]; o += fw
    fv = w_in[:, o:o + fw]; o += fw
    ff = w_in[:, o:o + N_FOX_HEADS]; o += N_FOX_HEADS
    nq = w_in[:, o:o + fw]; o += fw
    kv = w_in[:, o:o + 6 * kvw]; o += 6 * kvw
    ng = w_in[:, o:o + 3 * N_NSA_HEADS]
    d = w_in.shape[0]
    z64 = jnp.zeros((d, HEAD_DIM), w_in.dtype)
    fq_x, nq_x = [], []
    for h in range(N_FOX_HEADS):
        qh = fq[:, h * HEAD_DIM:(h + 1) * HEAD_DIM]
        fq_x += [qh, z64] if h % 2 == 0 else [z64, qh]
    for h in range(N_NSA_HEADS):
        qh = nq[:, h * HEAD_DIM:(h + 1) * HEAD_DIM]
        fq_pos = h // NSA_GROUP
        nq_x += [qh, z64] if fq_pos == 0 else [z64, qh]
    misc = jnp.concatenate([ff, ng, jnp.zeros((d, LANES - N_FOX_HEADS - 3 * N_NSA_HEADS), w_in.dtype)], axis=1)
    w = jnp.concatenate(fq_x + nq_x + [fk, fv, kv, misc], axis=1).astype(BF16)
    bias = jnp.concatenate([b_fox_f, b_nsa_gate,
                            jnp.zeros((LANES - N_FOX_HEADS - 3 * N_NSA_HEADS,), F32)]).reshape(1, LANES)
    return w, bias


def _bf16_pieces(x, n):
    out, r = [], float(x)
    for _ in range(n):
        p = float(np.asarray(r, F32).astype(BF16))
        out.append(p)
        r -= p
    return out


def _query_constants():
    fox = np.zeros((N_FOX_HEADS, LANES), np.float32)
    nsa = np.zeros((N_NSA_HEADS, LANES), np.float32)
    for h in range(N_FOX_HEADS):
        spare = HEAD_DIM * (1 - h % 2)
        fox[h, spare:spare + N_BIAS_LANES] = 1.0
    for h in range(N_NSA_HEADS):
        spare = HEAD_DIM * (1 - h // NSA_GROUP)
        for i, piece in enumerate(_bf16_pieces(_slope(h) * LOG2E, N_BIAS_LANES)):
            nsa[h, spare + 2 * i:spare + 2 * i + 2] = piece
    return jnp.asarray(np.stack([fox.reshape(-1), nsa.reshape(-1)]))


def _proj_kernel(x_ref, g_ref, w_ref, bias_ref, gains_ref, hm_ref, qc_ref,
                 fq_ref, nq_ref, fk_ref, fv_ref, ck_ref, cv_ref, sk_ref, sv_ref, wk_ref, wv_ref, misc_ref):
    x = x_ref[...]
    ms = jnp.mean(x * x, axis=-1, keepdims=True)
    xn = (x * lax.rsqrt(ms + RMS_EPS)) * g_ref[...]
    z = _dot(xn.astype(BF16), w_ref[...])
    hm = hm_ref[...]
    scale = HEAD_DIM ** -0.5 * LOG2E
    fq_ref[...] = (_head_rms(z[:, _FQ0:_NQ0], gains_ref[0:1, :], hm) * scale + qc_ref[0:1, :]).astype(BF16)
    nq_ref[...] = (_head_rms(z[:, _NQ0:_FK0], gains_ref[1:2, :], hm) * scale + qc_ref[1:2, :]).astype(BF16)
    fk_ref[...] = _head_rms(z[:, _FK0:_FV0], gains_ref[2:3, :], hm)
    fv_ref[...] = z[:, _FV0:_KV0]
    kv = z[:, _KV0:_MISC0]
    ck_ref[...] = kv[:, 0:128]
    cv_ref[...] = kv[:, 128:256]
    sk_ref[...] = _head_rms(kv[:, 256:384], gains_ref[3:4, :], hm)
    sv_ref[...] = kv[:, 384:512]
    wk_ref[...] = _head_rms(kv[:, 512:640], gains_ref[4:5, :], hm)
    wv_ref[...] = kv[:, 640:768]
    zb = z[:, _MISC0:_PROJ_COLS] + bias_ref[...]
    soft = jnp.log1p(jnp.exp(-jnp.abs(zb)))
    logsig = jnp.minimum(zb, 0.0) - soft
    lane = lax.broadcasted_iota(jnp.int32, zb.shape, 1)
    misc_ref[...] = jnp.where(lane < N_FOX_HEADS, logsig, jnp.exp(logsig))


def _project(x2d, g_attn, w, bias, gains, hm, tm):
    n, d = x2d.shape
    row = lambda c: pl.BlockSpec((tm, c), lambda i: (i, 0))
    full = lambda a: pl.BlockSpec(a.shape, lambda i: (0,) * a.ndim)
    g2 = g_attn.reshape(1, d)
    qc = _query_constants()
    out_shapes = ([jax.ShapeDtypeStruct((n, 1024), BF16)] * 2 + [jax.ShapeDtypeStruct((n, 512), F32)] * 2
                  + [jax.ShapeDtypeStruct((n, LANES), F32)] * 7)
    out_specs = [row(1024), row(1024), row(512), row(512)] + [row(LANES)] * 7
    return pl.pallas_call(
        _proj_kernel, grid=(n // tm,),
        in_specs=[row(d), full(g2), full(w), full(bias), full(gains), full(hm), full(qc)],
        out_specs=out_specs, out_shape=out_shapes,
        compiler_params=_cparams(("parallel",)), name="proj",
    )(x2d, g2, w, bias, gains, hm, qc)


def _dot_x3(x, m):
    hi, mid, lo = _split3(x)
    return _dot(hi, m) + _dot(mid, m) + _dot(lo, m)


def _cumsum_kernel(x_ref, u_ref, sl_ref, o_ref):
    local = _dot_x3(x_ref[0], u_ref[...])
    hi, mid, lo = _split3(jnp.broadcast_to(local[:, LANES - 1:LANES], local.shape))
    sl = sl_ref[...]
    o_ref[0] = local + (_dot(sl, hi) + _dot(sl, mid) + _dot(sl, lo))


def _cumsum_lanes(xt):
    b, h, s = xt.shape
    nchunk = s // LANES
    rows = h * nchunk
    i = np.arange(LANES)
    u = jnp.asarray(i[:, None] <= i[None, :], BF16)
    r = np.arange(rows)
    sl = jnp.asarray((r[:, None] // nchunk == r[None, :] // nchunk) & (r[None, :] < r[:, None]), BF16)
    spec = pl.BlockSpec((1, rows, LANES), lambda i: (i, 0, 0))
    out = pl.pallas_call(
        _cumsum_kernel, grid=(b,),
        in_specs=[spec, pl.BlockSpec((LANES, LANES), lambda i: (0, 0)), pl.BlockSpec((rows, rows), lambda i: (0, 0))],
        out_specs=spec, out_shape=jax.ShapeDtypeStruct((b, rows, LANES), F32),
        compiler_params=_cparams(("parallel",)), name="cumsum",
    )(xt.reshape(b, rows, LANES), u, sl)
    return out.reshape(b, h, s)


def _tri_pairs(n):
    qi = np.concatenate([np.full(i + 1, i) for i in range(n)]).astype(np.int32)
    ki = np.concatenate([np.arange(i + 1) for i in range(n)]).astype(np.int32)
    return jnp.asarray(qi), jnp.asarray(ki)


def _softmax_step(s, v, m_ref, l_ref, acc_ref, idx):
    m_prev = m_ref[idx]
    m_new = jnp.maximum(m_prev, jnp.max(s, axis=1, keepdims=True))
    alpha = jnp.exp(m_prev - m_new)
    p = jnp.exp(s - m_new)
    l_ref[idx] = alpha * l_ref[idx] + jnp.sum(p, axis=1, keepdims=True)
    acc_ref[idx] = alpha * acc_ref[idx] + _dot(p.astype(BF16), v)
    m_ref[idx] = m_new


def _init_softmax(m_ref, l_ref, acc_ref):
    m_ref[...] = jnp.full(m_ref.shape, NEG, F32)
    l_ref[...] = jnp.zeros(l_ref.shape, F32)
    acc_ref[...] = jnp.zeros(acc_ref.shape, F32)


def _exp2_step(s, v, m_ref, l_ref, acc_ref, idx):
    cols = s.shape[1] // LANES
    m_prev = m_ref[idx]
    m_new = jnp.maximum(m_prev, jnp.max(s, axis=1, keepdims=True))
    alpha = jnp.exp2(m_prev - m_new)
    p = jnp.exp2(s - jnp.tile(m_new, (1, cols)))
    l_ref[idx] = alpha * l_ref[idx] + sum(p[:, c * LANES:(c + 1) * LANES] for c in range(cols))
    acc_ref[idx] = alpha * acc_ref[idx] + _dot(p.astype(BF16), v)
    m_ref[idx] = m_new


def _softmax_result(l_ref, acc_ref, idx):
    return acc_ref[idx] / jnp.sum(l_ref[idx], axis=1, keepdims=True)


def _lower_tri(t):
    return lax.broadcasted_iota(jnp.int32, (t, t), 1) <= lax.broadcasted_iota(jnp.int32, (t, t), 0)


def _fox_bias_kernel(x_ref, lt_ref, sl_ref, pm_ref, a_ref, c_ref, tot_ref):
    nchunk = tot_ref.shape[0]
    lt = lt_ref[...]

    def local(i, carry):
        rows = pl.ds(pl.multiple_of(i * LANES, LANES), LANES)
        hi, mid, lo = _split3(x_ref[0, rows, :])
        c = _dot(lt, hi) + _dot(lt, mid) + _dot(lt, lo)
        c_ref[rows, :] = c
        tot_ref[pl.ds(i, 1), :] = c[LANES - 1:LANES, :]
        return carry

    lax.fori_loop(0, nchunk, local, 0)
    hi, mid, lo = _split3(tot_ref[...])
    sl = sl_ref[...]
    tot_ref[...] = _dot(sl, hi) + _dot(sl, mid) + _dot(sl, lo)

    def place(i, carry):
        rows = pl.ds(pl.multiple_of(i * LANES, LANES), LANES)
        hi, mid, lo = _split3((c_ref[rows, :] + tot_ref[pl.ds(i, 1), :]) * (-LOG2E))
        for hp in range(a_ref.shape[1]):
            a = _dot(hi, pm_ref[hp, 0]) + _dot(mid, pm_ref[hp, 1]) + _dot(lo, pm_ref[hp, 2])
            a_ref[0, hp, rows, :] = a.astype(BF16)
        return carry

    lax.fori_loop(0, nchunk, place, 0)


def _fox_bias(misc):
    b, s, _ = misc.shape
    nchunk = s // LANES
    i = np.arange(LANES)
    lt = jnp.asarray(i[:, None] >= i[None, :], BF16)
    c = np.arange(nchunk)
    sl = jnp.asarray(c[:, None] > c[None, :], BF16)
    pm = np.zeros((N_FOX_HEADS // 2, N_BIAS_LANES, LANES, LANES), np.float32)
    for hp in range(N_FOX_HEADS // 2):
        for piece in range(N_BIAS_LANES):
            pm[hp, piece, 2 * hp, HEAD_DIM + piece] = 1.0
            pm[hp, piece, 2 * hp + 1, piece] = 1.0
    pm = jnp.asarray(pm, BF16)
    full = lambda a: pl.BlockSpec(a.shape, lambda i: (0,) * a.ndim)
    return pl.pallas_call(
        _fox_bias_kernel, grid=(b,),
        in_specs=[pl.BlockSpec((1, s, LANES), lambda i: (i, 0, 0)), full(lt), full(sl), full(pm)],
        out_specs=pl.BlockSpec((1, N_FOX_HEADS // 2, s, LANES), lambda i: (i, 0, 0, 0)),
        out_shape=jax.ShapeDtypeStruct((b, N_FOX_HEADS // 2, s, LANES), BF16),
        scratch_shapes=[pltpu.VMEM((s, LANES), F32), pltpu.VMEM((nchunk, LANES), F32)],
        compiler_params=_cparams(("parallel",)), name="fox_bias",
    )(misc, lt, sl, pm)


def _fox_kernel(qt_ref, kt_ref, q_ref, k_ref, v_ref, a_ref, o_ref, m_ref, l_ref, acc_ref, *, t):
    step = pl.program_id(2)
    qi = qt_ref[step]
    ki = kt_ref[step]

    @pl.when(ki == 0)
    def _():
        _init_softmax(m_ref, l_ref, acc_ref)

    def update(diag):
        k = k_ref[0].astype(BF16)
        v = v_ref[0].astype(BF16)
        a = a_ref[0, 0]
        lane = lax.broadcasted_iota(jnp.int32, k.shape, 1)
        keys = (jnp.where(lane < HEAD_DIM, k, a), jnp.where(lane < HEAD_DIM, a, k))
        for h in range(2):
            s = _dot_nt(q_ref[0, :, h * LANES:(h + 1) * LANES], keys[h])
            if diag:
                s = jnp.where(_lower_tri(t), s, NEG)
            _exp2_step(s, v, m_ref, l_ref, acc_ref, h)

    @pl.when(ki < qi)
    def _():
        update(False)

    @pl.when(ki == qi)
    def _():
        update(True)
        lane = lax.broadcasted_iota(jnp.int32, (t, LANES), 1)
        o = jnp.where(lane < HEAD_DIM, _softmax_result(l_ref, acc_ref, 0), _softmax_result(l_ref, acc_ref, 1))
        o_ref[0] = o.astype(BF16)


def _fox_prompt(fq_x, fk, fv, bias, t):
    b, s, _ = fk.shape
    qt, kt = _tri_pairs(s // t)
    grid_spec = pltpu.PrefetchScalarGridSpec(
        num_scalar_prefetch=2, grid=(b, N_FOX_HEADS // 2, qt.shape[0]),
        in_specs=[
            pl.BlockSpec((1, t, 2 * LANES), lambda b, hp, i, qt, kt: (b, qt[i], hp)),
            pl.BlockSpec((1, t, LANES), lambda b, hp, i, qt, kt: (b, kt[i], hp)),
            pl.BlockSpec((1, t, LANES), lambda b, hp, i, qt, kt: (b, kt[i], hp)),
            pl.BlockSpec((1, 1, t, LANES), lambda b, hp, i, qt, kt: (b, hp, kt[i], 0)),
        ],
        out_specs=pl.BlockSpec((1, t, LANES), lambda b, hp, i, qt, kt: (b, qt[i], hp)),
        scratch_shapes=[pltpu.VMEM((2, t, LANES), F32)] * 3,
    )
    return pl.pallas_call(
        functools.partial(_fox_kernel, t=t), grid_spec=grid_spec,
        out_shape=jax.ShapeDtypeStruct((b, s, 512), BF16),
        compiler_params=_cparams(("parallel", "parallel", "arbitrary")), name="fox_prompt",
    )(qt, kt, fq_x, fk, fv, bias)


def _cmp_weights(w_cmp, pe):
    wl = w_cmp.reshape(CMP_BLOCK, HEAD_DIM, HEAD_DIM)
    eye = jnp.eye(N_NSA_KV, dtype=w_cmp.dtype)
    half = CMP_BLOCK // 2
    w2 = jnp.einsum('lde,gh->lgdhe', wl, eye).reshape(2, half * N_NSA_KV * HEAD_DIM, N_NSA_KV * HEAD_DIM)
    pe2 = jnp.tile(pe.reshape(2, half, 1, HEAD_DIM), (1, 1, N_NSA_KV, 1)).reshape(2, half * N_NSA_KV * HEAD_DIM)
    return w2.astype(BF16), pe2


def _compress(x, w_ref, pe_ref, i0):
    n = x.shape[0]
    a = _dot((x + pe_ref[i0:i0 + 1, :]).astype(BF16), w_ref[i0])
    b = _dot((x + pe_ref[i0 + 1:i0 + 2, :]).astype(BF16), w_ref[i0 + 1])
    return a + pltpu.roll(b, n - 1, axis=0)


def _cmp_kernel(xk_ref, xv_ref, w_ref, pe_ref, g_ref, hm_ref, kc_ref, vc_ref):
    kc = _compress(xk_ref[0], w_ref, pe_ref, 0)
    kc_ref[0] = _head_rms(kc, g_ref[...], hm_ref[...]).astype(BF16)
    vc_ref[0] = _compress(xv_ref[0], w_ref, pe_ref, 2).astype(BF16)


def _cmp_prompt(ck, cv, w4, pe4, g_k_cmp2, hm):
    b, s, _ = ck.shape
    n = s // CMP_STRIDE
    wide = CMP_STRIDE * LANES
    xspec = pl.BlockSpec((1, n, wide), lambda i: (i, 0, 0))
    full = lambda a: pl.BlockSpec(a.shape, lambda i: (0,) * a.ndim)
    ospec = pl.BlockSpec((1, n, LANES), lambda i: (i, 0, 0))
    return pl.pallas_call(
        _cmp_kernel, grid=(b,),
        in_specs=[xspec, xspec, full(w4), full(pe4), full(g_k_cmp2), full(hm)],
        out_specs=[ospec, ospec], out_shape=[jax.ShapeDtypeStruct((b, n, LANES), BF16)] * 2,
        compiler_params=_cparams(("parallel",)), name="cmp_prompt",
    )(ck.reshape(b, n, wide), cv.reshape(b, n, wide), w4, pe4, g_k_cmp2, hm)


def _slope(h):
    return float(2.0 ** (-8.0 * (h + 1) / N_NSA_HEADS))


def _overlap_matrix(nc, nsb):
    c0 = np.arange(nc)[:, None] * CMP_STRIDE
    s0 = np.arange(nsb)[None, :] * SEL_BLOCK
    return jnp.asarray((c0 < s0 + SEL_BLOCK) & (c0 + CMP_BLOCK > s0), BF16)


def _select_blocks(imp, t, nsb):
    jb = lax.broadcasted_iota(jnp.int32, imp.shape, 1)
    cur = t // SEL_BLOCK
    visible = jb * SEL_BLOCK <= t
    work = jnp.where((jb == 0) | (jb == cur) | (jb == cur - 1), FORCE_SCORE, imp)
    work = jnp.where(visible, work, NEG)
    jbf = jb.astype(F32)
    sel = jnp.zeros(imp.shape, F32)
    for _ in range(min(N_SELECT, nsb)):
        mx = jnp.max(work, axis=1, keepdims=True)
        first = jnp.min(jnp.where(work == mx, jbf, float(nsb)), axis=1, keepdims=True)
        pick = jbf == first
        sel = jnp.where(pick, 1.0, sel)
        work = jnp.where(pick, 2.0 * NEG, work)
    return jnp.where(visible, sel, 0.0)


def _cmp_attend(q, kc, vc, d, slope):
    valid = d >= 0.0
    s = jnp.where(valid, _dot_nt(q, kc) * LN2 - slope * d, NEG)
    m = jnp.max(s, axis=1, keepdims=True)
    e = jnp.where(valid, jnp.exp(s - m), 0.0)
    p = e / jnp.maximum(jnp.sum(e, axis=1, keepdims=True), TINY)
    return p, _dot(p.astype(BF16), vc)


def _data_lanes(q, g):
    lane = lax.broadcasted_iota(jnp.int32, q.shape, 1)
    return jnp.where(lane // HEAD_DIM == g, q, jnp.zeros_like(q))


def _nsa_cmp_kernel(q_ref, kc_ref, vc_ref, ov_ref, oc_ref, unsel_ref, *, tq, nc_valid):
    qi = pl.program_id(1)
    ncp = kc_ref.shape[1]
    nsbp = ov_ref.shape[1]
    t = qi * tq + lax.broadcasted_iota(jnp.int32, (tq, 1), 0)
    n = lax.broadcasted_iota(jnp.int32, (1, ncp), 1)
    cend = jnp.where(n < nc_valid, n * CMP_STRIDE + CMP_BLOCK - 1, jnp.int32(2 ** 30))
    d = (t - cend).astype(F32)
    kc = kc_ref[0]
    vc = vc_ref[0]
    ov = ov_ref[...]
    for g in range(N_NSA_KV):
        psum = jnp.zeros((tq, ncp), F32)
        for j in range(NSA_GROUP):
            h = g * NSA_GROUP + j
            p, o = _cmp_attend(_data_lanes(q_ref[0, :, h * LANES:(h + 1) * LANES], g), kc, vc, d, _slope(h))
            oc_ref[0, :, h * LANES:(h + 1) * LANES] = o
            psum = psum + p
        unsel_ref[0, g] = (1.0 - _select_blocks(_dot_x2(psum, ov), t, nsbp)).astype(BF16)


def _nsa_cmp_prompt(nq_x, kc, vc, tq):
    b, s, _ = nq_x.shape
    ncp = kc.shape[1]
    nc_valid = (s - CMP_BLOCK) // CMP_STRIDE + 1
    nsb = -(-s // SEL_BLOCK)
    assert nsb <= LANES, "the block map rides in one 128-lane contraction tile"
    ov = _overlap_matrix(ncp, LANES)
    return pl.pallas_call(
        functools.partial(_nsa_cmp_kernel, tq=tq, nc_valid=nc_valid), grid=(b, s // tq),
        in_specs=[pl.BlockSpec((1, tq, 1024), lambda b, i: (b, i, 0)),
                  pl.BlockSpec((1, ncp, LANES), lambda b, i: (b, 0, 0)),
                  pl.BlockSpec((1, ncp, LANES), lambda b, i: (b, 0, 0)),
                  pl.BlockSpec((ncp, LANES), lambda b, i: (0, 0))],
        out_specs=[pl.BlockSpec((1, tq, 1024), lambda b, i: (b, i, 0)),
                   pl.BlockSpec((1, N_NSA_KV, tq, LANES), lambda b, i: (b, 0, i, 0))],
        out_shape=[jax.ShapeDtypeStruct((b, s, 1024), F32), jax.ShapeDtypeStruct((b, N_NSA_KV, s, LANES), BF16)],
        compiler_params=_cparams(("parallel", "parallel")), name="nsa_cmp_prompt",
    )(nq_x, kc, vc, ov)


def _pack_heads(o_a, o_b, g):
    lane = lax.broadcasted_iota(jnp.int32, o_a.shape, 1)
    first = jnp.where(lane < HEAD_DIM, o_a, pltpu.roll(o_b, HEAD_DIM, axis=1))
    second = jnp.where(lane < HEAD_DIM, pltpu.roll(o_a, HEAD_DIM, axis=1), o_b)
    return jnp.where(g == 0, first, second)


def _nsa_mix_kernel(qt_ref, kt_ref, q_ref, sk_ref, sv_ref, unsel_ref, wkp_ref, wvp_ref, wkc_ref, wvc_ref,
                    oc_ref, misc_ref, o_ref, m_ref, l_ref, acc_ref, *, t):
    g = pl.program_id(1)
    step = pl.program_id(2)
    qi = qt_ref[step]
    ki = kt_ref[step]

    @pl.when(ki == 0)
    def _():
        _init_softmax(m_ref, l_ref, acc_ref)

    def update(diag):
        k = sk_ref[0].astype(BF16)
        v = sv_ref[0].astype(BF16)
        lane = lax.broadcasted_iota(jnp.int32, (t, LANES), 1)
        row = lax.broadcasted_iota(jnp.int32, (t, LANES), 0)
        pos = ki * t + row
        spare = lane - HEAD_DIM * (1 - g)
        pos_lo = pos & 255
        pos_part = jnp.where((spare & 1) == 0, pos - pos_lo, pos_lo).astype(F32).astype(BF16)
        keys = jnp.where((spare >= 0) & (spare < 2 * N_BIAS_LANES), pos_part, k)
        blk = ki * (t // SEL_BLOCK) + row // SEL_BLOCK
        rhs = jnp.concatenate([keys, jnp.where(lane == blk, MASK_SCORE, 0.0).astype(BF16)], axis=1)
        unsel = unsel_ref[0, 0]
        for j in range(NSA_GROUP):
            lhs = jnp.concatenate([q_ref[0, :, j * LANES:(j + 1) * LANES], unsel], axis=1)
            s = _dot_nt(lhs, rhs)
            if diag:
                s = jnp.where(_lower_tri(t), s, NEG)
            _exp2_step(s, v, m_ref, l_ref, acc_ref, j)

    @pl.when(ki < qi)
    def _():
        update(False)

    @pl.when(ki == qi)
    def _():
        update(True)
        kw = jnp.concatenate([wkp_ref[0], wkc_ref[0]], axis=0).astype(BF16)
        vw = jnp.concatenate([wvp_ref[0], wvc_ref[0]], axis=0).astype(BF16)
        nw = 2 * WINDOW
        dw = WINDOW + lax.broadcasted_iota(jnp.int32, (WINDOW, nw), 0) - lax.broadcasted_iota(jnp.int32, (WINDOW, nw), 1)
        band = (dw >= 0) & (dw <= WINDOW)
        dwf = dw.astype(F32)
        gates = misc_ref[0]
        outs = []
        for j in range(NSA_GROUP):
            slope = jnp.where(g == 0, _slope(j), _slope(NSA_GROUP + j))
            q = _data_lanes(q_ref[0, :, j * LANES:(j + 1) * LANES], g)
            strips = []
            for r in range(t // WINDOW):
                kp = qi * t + (r - 1) * WINDOW + lax.broadcasted_iota(jnp.int32, (1, nw), 1)
                keys = slice(r * WINDOW, r * WINDOW + nw)
                s = _dot_nt(q[r * WINDOW:(r + 1) * WINDOW], kw[keys]) * LN2 - slope * dwf
                s = jnp.where(band & (kp >= 0), s, NEG)
                e = jnp.exp(s - jnp.max(s, axis=1, keepdims=True))
                strips.append(_dot(e.astype(BF16), vw[keys]) / jnp.sum(e, axis=1, keepdims=True))
            o_w = jnp.concatenate(strips, axis=0)
            c0 = N_FOX_HEADS + 3 * j
            c1 = c0 + 3 * NSA_GROUP
            gate = [jnp.where(g == 0, gates[:, c0 + c:c0 + c + 1], gates[:, c1 + c:c1 + c + 1]) for c in range(3)]
            outs.append(gate[0] * oc_ref[0, :, j * LANES:(j + 1) * LANES]
                        + gate[1] * _softmax_result(l_ref, acc_ref, j) + gate[2] * o_w)
        for c in range(NSA_GROUP // 2):
            o_ref[0, :, c * LANES:(c + 1) * LANES] = _pack_heads(outs[2 * c], outs[2 * c + 1], g).astype(BF16)


def _nsa_mix_prompt(nq_x, sk, sv, unsel, wk, wv, oc_x, misc, t):
    b, s, _ = nq_x.shape
    assert t % WINDOW == 0
    qt, kt = _tri_pairs(s // t)
    gw = NSA_GROUP * LANES
    qmap = lambda b, g, i, qt, kt: (b, qt[i], 0)
    gmap = lambda b, g, i, qt, kt: (b, qt[i], g)
    kmap = lambda b, g, i, qt, kt: (b, kt[i], 0)
    pmap = lambda b, g, i, qt, kt: (b, jnp.maximum(qt[i] * (t // WINDOW) - 1, 0), 0)
    grid_spec = pltpu.PrefetchScalarGridSpec(
        num_scalar_prefetch=2, grid=(b, N_NSA_KV, qt.shape[0]),
        in_specs=[
            pl.BlockSpec((1, t, gw), gmap),
            pl.BlockSpec((1, t, LANES), kmap), pl.BlockSpec((1, t, LANES), kmap),
            pl.BlockSpec((1, 1, t, LANES), lambda b, g, i, qt, kt: (b, g, qt[i], 0)),
            pl.BlockSpec((1, WINDOW, LANES), pmap), pl.BlockSpec((1, WINDOW, LANES), pmap),
            pl.BlockSpec((1, t, LANES), qmap), pl.BlockSpec((1, t, LANES), qmap),
            pl.BlockSpec((1, t, gw), gmap),
            pl.BlockSpec((1, t, LANES), qmap),
        ],
        out_specs=pl.BlockSpec((1, t, gw // 2), gmap),
        scratch_shapes=[pltpu.VMEM((NSA_GROUP, t, LANES), F32)] * 3,
    )
    return pl.pallas_call(
        functools.partial(_nsa_mix_kernel, t=t), grid_spec=grid_spec,
        out_shape=jax.ShapeDtypeStruct((b, s, 512), BF16),
        compiler_params=_cparams(("parallel", "parallel", "arbitrary")), name="nsa_mix_prompt",
    )(qt, kt, nq_x, sk, sv, unsel, wk, wv, wk, wv, oc_x, misc)


_ROUTE_E, _ROUTE_GATE, _ROUTE_RANK = 0, TOP_K, 2 * TOP_K


def _router_weights(w_router, b_router):
    d = w_router.shape[0]
    wp = jnp.concatenate([w_router, jnp.zeros((d, LANES - N_EXPERTS), F32)], axis=1)
    hi = wp.astype(BF16)
    lo = (wp - hi.astype(F32)).astype(BF16)
    bias = jnp.concatenate([b_router.astype(F32), jnp.full((LANES - N_EXPERTS,), NEG, F32)]).reshape(1, LANES)
    return hi, lo, bias


def _tail_kernel(x_ref, mf_ref, mn_ref, wo_ref, g_ref, wrh_ref, wrl_ref, br_ref, tri_ref,
                 x1_ref, h_ref, route_ref, cnt_ref):
    @pl.when(pl.program_id(0) == 0)
    def _():
        cnt_ref[...] = jnp.zeros(cnt_ref.shape, F32)

    half = mf_ref.shape[1]
    x1 = x_ref[...] + _dot(mf_ref[...], wo_ref[0:half, :]) + _dot(mn_ref[...], wo_ref[half:2 * half, :])
    x1_ref[...] = x1
    h = (x1 * lax.rsqrt(jnp.mean(x1 * x1, axis=-1, keepdims=True) + RMS_EPS)) * g_ref[...]
    h_hi, h_lo = _split2(h)
    bits = lax.bitcast_convert_type(h_hi.astype(F32), jnp.uint32)
    words = bits.shape[1] // 2
    h_ref[...] = bits[:, :words] | (bits[:, words:] >> 16)
    logits = _dot(h_hi, wrh_ref[...]) + _dot(h_lo, wrh_ref[...]) + _dot(h_hi, wrl_ref[...]) + br_ref[...]
    lane = lax.broadcasted_iota(jnp.int32, logits.shape, 1)
    lanef = lane.astype(F32)
    work = logits
    picks, vals, ids = [], [], []
    for _ in range(TOP_K):
        mx = jnp.max(work, axis=1, keepdims=True)
        first = jnp.min(jnp.where(work == mx, lanef, float(LANES)), axis=1, keepdims=True)
        pick = lanef == first
        work = jnp.where(pick, 2.0 * NEG, work)
        picks.append(pick); vals.append(mx); ids.append(first)
    ex = [jnp.exp(v - vals[0]) for v in vals]
    den = ex[0] + ex[1] + ex[2] + ex[3]
    onehot = jnp.zeros(logits.shape, F32)
    for p in picks:
        onehot = jnp.where(p, 1.0, onehot)
    carry = cnt_ref[0:1, :]
    ranks = _dot(tri_ref[...], onehot.astype(BF16)) + carry
    route = jnp.zeros(logits.shape, F32)
    for k in range(TOP_K):
        rank_k = jnp.sum(jnp.where(picks[k], ranks, 0.0), axis=1, keepdims=True)
        route = jnp.where(lane == _ROUTE_E + k, ids[k], route)
        route = jnp.where(lane == _ROUTE_GATE + k, ex[k] / den, route)
        route = jnp.where(lane == _ROUTE_RANK + k, rank_k, route)
    route_ref[...] = route
    cnt_ref[...] = jnp.broadcast_to(carry + jnp.sum(onehot, axis=0, keepdims=True), cnt_ref.shape)


def _tail(x2d, mf, mn, wo, g_ffn, wrh, wrl, br, tm):
    n, d = x2d.shape
    i = np.arange(tm)
    tri = jnp.asarray(i[:, None] > i[None, :], BF16)
    row = lambda c: pl.BlockSpec((tm, c), lambda i: (i, 0))
    full = lambda a: pl.BlockSpec(a.shape, lambda i: (0,) * a.ndim)
    g2 = g_ffn.reshape(1, d)
    return pl.pallas_call(
        _tail_kernel, grid=(n // tm,),
        in_specs=[row(d), row(mf.shape[1]), row(mn.shape[1]), full(wo), full(g2), full(wrh), full(wrl), full(br),
                  full(tri)],
        out_specs=[row(d), row(d // 2), row(LANES), pl.BlockSpec((8, LANES), lambda i: (0, 0))],
        out_shape=[jax.ShapeDtypeStruct((n, d), F32), jax.ShapeDtypeStruct((n, d // 2), jnp.uint32),
                   jax.ShapeDtypeStruct((n, LANES), F32), jax.ShapeDtypeStruct((8, LANES), F32)],
        compiler_params=_cparams(("arbitrary",)), name="tail",
    )(x2d, mf, mn, wo, g2, wrh, wrl, br, tri)


def _row_copy(src_hbm, src_row, dst, dst_row, sem):
    return pltpu.make_async_copy(src_hbm.at[pl.ds(src_row, 1)], dst.at[pl.ds(dst_row, 1)], sem)


def _dispatch_kernel(dest_ref, h_ref, xs_in_hbm, xs_hbm, sem, *, tm):
    del xs_in_hbm

    def issue(r, c):
        for k in range(TOP_K):
            _row_copy(h_ref, r, xs_hbm, dest_ref[0, 0, TOP_K * r + k], sem).start()
        return c

    def drain(r, c):
        for k in range(TOP_K):
            _row_copy(h_ref, r, xs_hbm, dest_ref[0, 0, TOP_K * r + k], sem).wait()
        return c

    lax.fori_loop(0, tm, issue, 0)
    lax.fori_loop(0, tm, drain, 0)


def _dispatch(h, dest, xs, tm):
    n, d = h.shape
    dest3 = dest.reshape(n // tm, 1, tm * TOP_K)
    return pl.pallas_call(
        functools.partial(_dispatch_kernel, tm=tm), grid=(n // tm,),
        in_specs=[pl.BlockSpec((1, 1, tm * TOP_K), lambda i: (i, 0, 0), memory_space=pltpu.SMEM),
                  pl.BlockSpec((tm, d), lambda i: (i, 0)), pl.BlockSpec(memory_space=pl.ANY)],
        out_specs=pl.BlockSpec(memory_space=pl.ANY),
        out_shape=jax.ShapeDtypeStruct(xs.shape, xs.dtype),
        scratch_shapes=[pltpu.SemaphoreType.DMA(())],
        input_output_aliases={2: 0},
        compiler_params=_cparams(("arbitrary",)), name="moe_dispatch",
    )(dest3, h, xs)


_PAIR = 2 * LANES


def _split_gate_up_kernel(w_ref, p_ref, g_ref, u_ref):
    p = p_ref[...]
    for c in range(w_ref.shape[2] // _PAIR):
        t = _dot(w_ref[0, :, c * _PAIR:(c + 1) * _PAIR].astype(BF16), p)
        g_ref[0, :, c * LANES:(c + 1) * LANES] = t[:, :LANES].astype(BF16)
        u_ref[0, :, c * LANES:(c + 1) * LANES] = t[:, LANES:].astype(BF16)


def _split_gate_up(w_gate_up, rows):
    e, d, de2 = w_gate_up.shape
    perm = np.zeros((_PAIR, _PAIR), np.float32)
    j = np.arange(LANES)
    perm[2 * j, j] = 1.0
    perm[2 * j + 1, LANES + j] = 1.0
    ospec = pl.BlockSpec((1, rows, de2 // 2), lambda i, r: (i, r, 0))
    return pl.pallas_call(
        _split_gate_up_kernel, grid=(e, d // rows),
        in_specs=[pl.BlockSpec((1, rows, de2), lambda i, r: (i, r, 0)),
                  pl.BlockSpec((_PAIR, _PAIR), lambda i, r: (0, 0))],
        out_specs=[ospec, ospec], out_shape=[jax.ShapeDtypeStruct((e, d, de2 // 2), BF16)] * 2,
        compiler_params=_cparams(("parallel", "parallel")), name="split_gate_up",
    )(w_gate_up, jnp.asarray(perm, BF16))


def _expert_kernel(be_ref, nused_ref, x_ref, wg_ref, wu_ref, bg_ref, bu_ref, wd_ref, bd_ref, y_ref):
    @pl.when(pl.program_id(0) < nused_ref[0])
    def _():
        words = x_ref[...]
        left = lax.bitcast_convert_type(words & jnp.uint32(0xFFFF0000), F32)
        right = lax.bitcast_convert_type(words << 16, F32)
        x = jnp.concatenate([left, right], axis=1).astype(BF16)
        g = jnp.minimum(_dot(x, wg_ref[0]) + bg_ref[0], SWIGLU_LIMIT)
        u = jnp.clip(_dot(x, wu_ref[0]) + bu_ref[0], -SWIGLU_LIMIT, SWIGLU_LIMIT)
        act = (u + 1.0) * g * jax.nn.sigmoid(SWIGLU_ALPHA * g)
        y_ref[...] = _dot(act.astype(BF16), wd_ref[0]) + bd_ref[0]

    @pl.when(pl.program_id(0) >= nused_ref[0])
    def _():
        y_ref[...] = jnp.zeros(y_ref.shape, F32)


def _experts(xs, blk_expert, nused, wg, wu, bg, bu, wd, bd, mb):
    n_slots, words = xs.shape
    _, d, de = wg.shape
    rows = lambda i, be, nu: (jnp.minimum(i, nu[0] - 1), 0)
    wmap = lambda i, be, nu: (be[i], 0, 0)
    grid_spec = pltpu.PrefetchScalarGridSpec(
        num_scalar_prefetch=2, grid=(n_slots // mb,),
        in_specs=[pl.BlockSpec((mb, words), rows),
                  pl.BlockSpec((1, d, de), wmap), pl.BlockSpec((1, d, de), wmap),
                  pl.BlockSpec((1, 1, de), wmap), pl.BlockSpec((1, 1, de), wmap),
                  pl.BlockSpec((1, de, d), wmap), pl.BlockSpec((1, 1, d), wmap)],
        out_specs=pl.BlockSpec((mb, d), lambda i, be, nu: (i, 0)),
    )
    return pl.pallas_call(
        _expert_kernel, grid_spec=grid_spec, out_shape=jax.ShapeDtypeStruct((n_slots, d), F32),
        compiler_params=_cparams(("arbitrary",)), name="moe_experts",
    )(blk_expert, nused, xs, wg, wu, bg, bu, wd, bd)


def _combine_kernel(dest_ref, route_ref, x1_ref, ys_hbm, o_ref, buf_ref, sem, *, tm):
    def issue(r, c):
        for k in range(TOP_K):
            _row_copy(ys_hbm, dest_ref[0, 0, TOP_K * r + k], buf_ref.at[k], r, sem).start()
        return c

    def drain(r, c):
        for k in range(TOP_K):
            _row_copy(ys_hbm, dest_ref[0, 0, TOP_K * r + k], buf_ref.at[k], r, sem).wait()
        return c

    lax.fori_loop(0, tm, issue, 0)
    lax.fori_loop(0, tm, drain, 0)
    route = route_ref[...]
    y = route[:, _ROUTE_GATE:_ROUTE_GATE + 1] * buf_ref[0]
    for k in range(1, TOP_K):
        y = y + route[:, _ROUTE_GATE + k:_ROUTE_GATE + k + 1] * buf_ref[k]
    o_ref[...] = x1_ref[...] + y


def _combine(x1, route, dest, ys, tm):
    n, d = x1.shape
    dest3 = dest.reshape(n // tm, 1, tm * TOP_K)
    row = lambda c: pl.BlockSpec((tm, c), lambda i: (i, 0))
    return pl.pallas_call(
        functools.partial(_combine_kernel, tm=tm), grid=(n // tm,),
        in_specs=[pl.BlockSpec((1, 1, tm * TOP_K), lambda i: (i, 0, 0), memory_space=pltpu.SMEM),
                  row(LANES), row(d), pl.BlockSpec(memory_space=pl.ANY)],
        out_specs=row(d), out_shape=jax.ShapeDtypeStruct((n, d), F32),
        scratch_shapes=[pltpu.VMEM((TOP_K, tm, d), F32), pltpu.SemaphoreType.DMA(())],
        compiler_params=_cparams(("arbitrary",)), name="moe_combine",
    )(dest3, route, x1, ys)


def _moe_plan(route_p, cnt_p, route_s, cnt_s, mb):
    n_tok = route_p.shape[0] + route_s.shape[0]
    cp = cnt_p[0, :N_EXPERTS].astype(jnp.int32)
    cs = cnt_s[0, :N_EXPERTS].astype(jnp.int32)
    counts = cp + cs
    padded = (counts + mb - 1) // mb * mb
    pend = jnp.cumsum(padded)
    pstart = pend - padded
    nblk = -(-n_tok * TOP_K // mb) + N_EXPERTS
    first_slot = jnp.arange(nblk, dtype=jnp.int32) * mb
    blk_expert = jnp.clip(jnp.sum(pend[None, :] <= first_slot[:, None], axis=1), 0, N_EXPERTS - 1).astype(jnp.int32)
    nused = (pend[-1:] // mb).astype(jnp.int32)

    def dest(route, first):
        e = route[:, _ROUTE_E:_ROUTE_E + TOP_K].astype(jnp.int32)
        rank = route[:, _ROUTE_RANK:_ROUTE_RANK + TOP_K].astype(jnp.int32)
        onehot = e[..., None] == jnp.arange(N_EXPERTS, dtype=jnp.int32)
        return jnp.sum(jnp.where(onehot, first, 0), axis=-1) + rank

    return dest(route_p, pstart), dest(route_s, pstart + cp), blk_expert, nused, nblk


def _page_spec(shape, n_pages, pp, j):
    return pl.BlockSpec((1,) + shape, lambda b, c, pt: (pt[b * n_pages + c * pp + j],) + (0,) * len(shape))


def _slope_col():
    h = lax.broadcasted_iota(jnp.int32, (N_NSA_HEADS, 1), 0).astype(F32)
    return jnp.exp2(-8.0 * (h + 1.0) / N_NSA_HEADS)


def _rounded(ref):
    return ref[0].astype(BF16).astype(F32)


def _append_key(q, s_bias, k_new, v_new, m_ref, l_ref, acc_ref):
    s = jnp.sum(q.astype(F32) * k_new, axis=1, keepdims=True) * LN2 + s_bias
    m_prev = m_ref[0]
    m_new = jnp.maximum(m_prev, s)
    alpha = jnp.exp(m_prev - m_new)
    p = jnp.exp(s - m_new)
    l_ref[0] = alpha * l_ref[0] + p
    acc_ref[0] = alpha * acc_ref[0] + p * v_new
    m_ref[0] = m_new


def _rows(h):
    return slice(h * HEAD_DIM, (h + 1) * HEAD_DIM)


def _page_scores(q, k, group):
    return jnp.concatenate([jnp.sum(k[_rows(h // group)] * q[_rows(h)], axis=0, keepdims=True)
                            for h in range(q.shape[0] // HEAD_DIM)], axis=0)


def _paged_update(s, v_refs, group, m_ref, l_ref, acc_ref):
    pp = len(v_refs)
    m_prev = m_ref[...]
    m_new = jnp.maximum(m_prev, jnp.max(s, axis=1, keepdims=True))
    alpha = jnp.exp2(m_prev - m_new)
    p = jnp.exp2(s - jnp.tile(m_new, (1, pp)))
    l_ref[...] = alpha * l_ref[...] + sum(p[:, j * LANES:(j + 1) * LANES] for j in range(pp))
    for h in range(s.shape[0]):
        upd = acc_ref[_rows(h), :] * alpha[h:h + 1, :]
        for j in range(pp):
            upd = upd + p[h:h + 1, j * LANES:(j + 1) * LANES] * v_refs[j][0, _rows(h // group), :]
        acc_ref[_rows(h), :] = upd
    m_ref[...] = m_new


def _paged_finish(q, s_bias, k_new, v_new, group, m_ref, l_ref, acc_ref, o_ref):
    s_new = _page_scores(q, k_new, group) + s_bias
    m_prev = m_ref[...]
    m_new = jnp.maximum(m_prev, s_new)
    alpha = jnp.exp2(m_prev - m_new)
    first = lax.broadcasted_iota(jnp.int32, (1, LANES), 1) == 0
    p_new = jnp.where(first, jnp.exp2(s_new - m_new), 0.0)
    l = alpha * l_ref[...] + p_new
    for h in range(s_new.shape[0]):
        acc = acc_ref[_rows(h), :] * alpha[h:h + 1, :] + p_new[h:h + 1, :] * v_new[_rows(h // group)]
        o = jnp.sum(acc, axis=1, keepdims=True) / jnp.sum(l[h:h + 1, :], axis=1, keepdims=True)
        o_ref[0, _rows(h), :] = jnp.broadcast_to(o, (HEAD_DIM, LANES))


def _fox_sample_kernel(pt_ref, q_ref, kn_ref, vn_ref, *refs, pp):
    del pt_ref
    k_refs, v_refs = refs[:pp], refs[pp:2 * pp]
    c_ref, cn_ref, o_ref, m_ref, l_ref, acc_ref = refs[2 * pp:]
    c = pl.program_id(1)

    @pl.when(c == 0)
    def _():
        _init_softmax(m_ref, l_ref, acc_ref)

    q = q_ref[0]
    s = jnp.concatenate([_page_scores(q, r[0], 1) for r in k_refs], axis=1) - c_ref[0]
    _paged_update(s, v_refs, 1, m_ref, l_ref, acc_ref)

    @pl.when(c == pl.num_programs(1) - 1)
    def _():
        _paged_finish(q, -cn_ref[0], kn_ref[0], vn_ref[0], 1, m_ref, l_ref, acc_ref, o_ref)


def _paged_call(kernel, name, pt, head_inputs, cache_k, cache_v, extra, extra_specs, pp):
    b, n_pages = pt.shape
    rows = cache_k.shape[1]
    bmap = lambda b, c, pt: (b, 0, 0)
    per_b = lambda a: pl.BlockSpec((1,) + a.shape[1:], bmap)
    out_rows = head_inputs[0].shape[1]
    grid_spec = pltpu.PrefetchScalarGridSpec(
        num_scalar_prefetch=1, grid=(b, n_pages // pp),
        in_specs=([per_b(a) for a in head_inputs]
                  + [_page_spec((rows, PAGE_SIZE), n_pages, pp, j) for j in range(pp)] * 2 + extra_specs),
        out_specs=pl.BlockSpec((1, out_rows, LANES), bmap),
        scratch_shapes=[pltpu.VMEM((out_rows // HEAD_DIM, LANES), F32), pltpu.VMEM((out_rows // HEAD_DIM, LANES), F32),
                        pltpu.VMEM((out_rows, LANES), F32)],
    )
    return pl.pallas_call(
        kernel, grid_spec=grid_spec, out_shape=jax.ShapeDtypeStruct((b, out_rows, LANES), F32),
        compiler_params=_cparams(("parallel", "arbitrary")), name=name,
    )(pt.reshape(-1), *head_inputs, *([cache_k] * pp), *([cache_v] * pp), *extra)


def _fox_sample(pt, q_rep, kn_rep, vn_rep, cache_kt, cache_vt, cs, c_new, pp):
    bmap = lambda b, c, pt: (b, 0, 0)
    specs = [pl.BlockSpec((1, N_FOX_HEADS, pp * PAGE_SIZE), lambda b, c, pt: (b, 0, c)),
             pl.BlockSpec((1, N_FOX_HEADS, LANES), bmap)]
    return _paged_call(functools.partial(_fox_sample_kernel, pp=pp), "fox_sample", pt, [q_rep, kn_rep, vn_rep],
                       cache_kt, cache_vt, [cs, c_new], specs, pp)


def _cmp_sample_kernel(pt_ref, q_ref, *refs, pp, past):
    del pt_ref
    xk_refs, xv_refs = refs[:pp], refs[pp:2 * pp]
    w_ref, pe_ref, g_ref, hm_ref, ov_ref, oc_ref, sel_ref, a_ref = refs[2 * pp:]
    c = pl.program_id(1)
    rpp = PAGE_SIZE // CMP_STRIDE
    xk = jnp.concatenate([r[0] for r in xk_refs], axis=0)
    xv = jnp.concatenate([r[0] for r in xv_refs], axis=0)
    rows = pl.ds(pl.multiple_of(c * (rpp * pp), rpp * pp), rpp * pp)
    for i, x in ((0, xk), (1, xk), (2, xv), (3, xv)):
        a_ref[i, rows, :] = _dot((x + pe_ref[i:i + 1, :]).astype(BF16), w_ref[i])

    @pl.when(c == pl.num_programs(1) - 1)
    def _():
        n = a_ref.shape[1]
        nsbp = ov_ref.shape[1]
        kc = _head_rms(a_ref[0] + pltpu.roll(a_ref[1], n - 1, axis=0), g_ref[...], hm_ref[...]).astype(BF16)
        vc = (a_ref[2] + pltpu.roll(a_ref[3], n - 1, axis=0)).astype(BF16)
        nc_valid = (past + 1 - CMP_BLOCK) // CMP_STRIDE + 1
        idx = lax.broadcasted_iota(jnp.int32, (N_NSA_HEADS, n), 1)
        cend = jnp.where(idx < nc_valid, idx * CMP_STRIDE + CMP_BLOCK - 1, jnp.int32(2 ** 30))
        p, o = _cmp_attend(q_ref[0], kc, vc, (past - cend).astype(F32), _slope_col())
        oc_ref[0] = o
        psum = jnp.concatenate([jnp.sum(p[0:NSA_GROUP], axis=0, keepdims=True),
                                jnp.sum(p[NSA_GROUP:2 * NSA_GROUP], axis=0, keepdims=True),
                                jnp.zeros((N_NSA_HEADS - N_NSA_KV, n), F32)], axis=0)
        t = jnp.full((N_NSA_HEADS, 1), past, jnp.int32)
        sel = _select_blocks(_dot_x2(psum, ov_ref[...]), t, nsbp)
        sel_ref[0] = jnp.concatenate([jnp.broadcast_to(sel[0:1], (NSA_GROUP, nsbp)),
                                      jnp.broadcast_to(sel[1:2], (NSA_GROUP, nsbp))], axis=0).astype(BF16)


def _cmp_sample(pt, qg, cache_ck, cache_cv, w4, pe4, g_k_cmp2, hm, pp):
    b, n_pages = pt.shape
    past = n_pages * PAGE_SIZE
    wide = cache_ck.shape[2]
    rpp = PAGE_SIZE // CMP_STRIDE
    n = n_pages * rpp
    nsb = -(-(past + 1) // SEL_BLOCK)
    nsbp = -(-nsb // LANES) * LANES
    ov = _overlap_matrix(n, nsbp)
    bmap = lambda b, c, pt: (b, 0, 0)
    full = lambda a: pl.BlockSpec(a.shape, lambda b, c, pt: (0,) * a.ndim)
    grid_spec = pltpu.PrefetchScalarGridSpec(
        num_scalar_prefetch=1, grid=(b, n_pages // pp),
        in_specs=([pl.BlockSpec((1, N_NSA_HEADS, LANES), bmap)]
                  + [_page_spec((rpp, wide), n_pages, pp, j) for j in range(pp)] * 2
                  + [full(w4), full(pe4), full(g_k_cmp2), full(hm), full(ov)]),
        out_specs=[pl.BlockSpec((1, N_NSA_HEADS, LANES), bmap), pl.BlockSpec((1, N_NSA_HEADS, nsbp), bmap)],
        scratch_shapes=[pltpu.VMEM((4, n, LANES), F32)],
    )
    return pl.pallas_call(
        functools.partial(_cmp_sample_kernel, pp=pp, past=past), grid_spec=grid_spec,
        out_shape=[jax.ShapeDtypeStruct((b, N_NSA_HEADS, LANES), F32),
                   jax.ShapeDtypeStruct((b, N_NSA_HEADS, nsbp), BF16)],
        compiler_params=_cparams(("parallel", "arbitrary")), name="cmp_sample",
    )(pt.reshape(-1), qg, *([cache_ck] * pp), *([cache_cv] * pp), w4, pe4, g_k_cmp2, hm, ov)


def _sel_sample_kernel(pt_ref, q_ref, kn_ref, vn_ref, *refs, pp, past):
    del pt_ref
    k_refs, v_refs = refs[:pp], refs[pp:2 * pp]
    sel_ref, o_ref, m_ref, l_ref, acc_ref = refs[2 * pp:]
    c = pl.program_id(1)
    tc = pp * PAGE_SIZE
    nsbp = sel_ref.shape[2]

    @pl.when(c == 0)
    def _():
        _init_softmax(m_ref, l_ref, acc_ref)

    q = q_ref[0]
    col = lax.broadcasted_iota(jnp.int32, (nsbp, tc), 1)
    blk = lax.broadcasted_iota(jnp.int32, (nsbp, tc), 0)
    expand = (blk == c * (tc // SEL_BLOCK) + col // SEL_BLOCK).astype(BF16)
    chosen = _dot(sel_ref[0], expand) > 0.5
    kpos = (c * tc - past + lax.broadcasted_iota(jnp.int32, (1, tc), 1)).astype(F32)
    s = jnp.concatenate([_page_scores(q, r[0], NSA_GROUP) for r in k_refs], axis=1) + (_slope_col() * LOG2E) * kpos
    _paged_update(jnp.where(chosen, s, NEG), v_refs, NSA_GROUP, m_ref, l_ref, acc_ref)

    @pl.when(c == pl.num_programs(1) - 1)
    def _():
        _paged_finish(q, 0.0, kn_ref[0], vn_ref[0], NSA_GROUP, m_ref, l_ref, acc_ref, o_ref)


def _sel_sample(pt, q_rep, kn_rep, vn_rep, cache_kt, cache_vt, selh, pp):
    past = pt.shape[1] * PAGE_SIZE
    specs = [pl.BlockSpec((1,) + selh.shape[1:], lambda b, c, pt: (b, 0, 0))]
    return _paged_call(functools.partial(_sel_sample_kernel, pp=pp, past=past), "sel_sample", pt,
                       [q_rep, kn_rep, vn_rep], cache_kt, cache_vt, [selh], specs, pp)


def _gate_sample_kernel(q_ref, os_ref, stk_ref, stv_ref, wkn_ref, wvn_ref, oc_ref, gate_ref, o_ref):
    q = q_ref[0]
    slope = _slope_col()
    nbuf = stk_ref.shape[1]
    dwin = (nbuf - lax.broadcasted_iota(jnp.int32, (1, nbuf), 1)).astype(F32)
    s_w = _dot_nt(q, stk_ref[0].astype(BF16)) * LN2 - slope * dwin
    s_n = jnp.sum(q.astype(F32) * _rounded(wkn_ref), axis=1, keepdims=True) * LN2
    mw = jnp.maximum(jnp.max(s_w, axis=1, keepdims=True), s_n)
    e = jnp.exp(s_w - mw)
    e_n = jnp.exp(s_n - mw)
    o_w = ((_dot(e.astype(BF16), stv_ref[0].astype(BF16)) + e_n * _rounded(wvn_ref))
           / (jnp.sum(e, axis=1, keepdims=True) + e_n))
    g = gate_ref[0]
    o_ref[0] = g[:, 0:1] * oc_ref[0] + g[:, 1:2] * os_ref[0] + g[:, 2:3] * o_w


def _gate_sample(qg, o_sel, st_k, st_v, wk_new, wv_new, oc, gates):
    b = qg.shape[0]
    spec = lambda a: pl.BlockSpec((1,) + a.shape[1:], lambda i: (i, 0, 0))
    args = (qg, o_sel, st_k, st_v, wk_new, wv_new, oc, gates)
    return pl.pallas_call(
        _gate_sample_kernel, grid=(b,), in_specs=[spec(a) for a in args], out_specs=spec(oc),
        out_shape=jax.ShapeDtypeStruct(oc.shape, F32),
        compiler_params=_cparams(("parallel",)), name="gate_sample",
    )(*args)


PROMPT_ROWS = 256
ATTN_TILE = 1024
CMP_Q_TILE = 1024
MOE_ROWS = 256
WEIGHT_ROWS = 512
PAGES_PER_STEP = 8


def _layer(xp, xs, c_fox_k, c_fox_v, c_fox_logf, c_cmp_k, c_cmp_v, c_sel_k, c_sel_v, st_win_k, st_win_v, page_table,
           g_attn, w_in, b_fox_f, b_nsa_gate, g_q_fox, g_k_fox, g_q_nsa, g_k_sel, g_k_win, g_k_cmp,
           pe_cmp_k, pe_cmp_v, w_cmp_k, w_cmp_v, w_out, g_ffn, w_router, b_router, w_gate_up, b_gate_up,
           w_down, b_down):
    bp, sp, d = xp.shape
    bs = xs.shape[0]
    n_pool = c_fox_k.shape[0]
    n_pages = page_table.shape[1]
    past = n_pages * PAGE_SIZE

    w, bias = _proj_weights(w_in, b_fox_f, b_nsa_gate)
    tile2 = lambda g: jnp.tile(g, 2)
    gains = jnp.stack([tile2(g_q_fox), tile2(g_q_nsa), tile2(g_k_fox), tile2(g_k_sel), tile2(g_k_win)])
    hm = _head_mean_matrix()
    wk2, pek = _cmp_weights(w_cmp_k, pe_cmp_k)
    wv2, pev = _cmp_weights(w_cmp_v, pe_cmp_v)
    w4 = jnp.concatenate([wk2, wv2])
    pe4 = jnp.concatenate([pek, pev])
    g_cmp2 = tile2(g_k_cmp).reshape(1, LANES)
    wo = w_out.astype(BF16)
    wrh, wrl, br = _router_weights(w_router, b_router)
    wg, wu = _split_gate_up(w_gate_up, WEIGHT_ROWS)
    bg = b_gate_up[:, None, 0::2]
    bu = b_gate_up[:, None, 1::2]
    wd = w_down.astype(BF16)
    bd = b_down[:, None, :]

    pr = _project(xp.reshape(bp * sp, d), g_attn, w, bias, gains, hm, PROMPT_ROWS)
    fq, nq, fk, fv, ck, cv, sk, sv, wk, wv, misc = [a.reshape(bp, sp, -1) for a in pr]
    logf = misc[:, :, :N_FOX_HEADS]
    mix_fox = _fox_prompt(fq, fk, fv, _fox_bias(misc), ATTN_TILE)
    kc, vc = _cmp_prompt(ck, cv, w4, pe4, g_cmp2, hm)
    oc, unsel = _nsa_cmp_prompt(nq, kc, vc, CMP_Q_TILE)
    mix_nsa = _nsa_mix_prompt(nq, sk, sv, unsel, wk, wv, oc, misc, ATTN_TILE)
    x1p, hp, route_p, cnt_p = _tail(xp.reshape(bp * sp, d), mix_fox.reshape(bp * sp, -1),
                                    mix_nsa.reshape(bp * sp, -1), wo, g_ffn, wrh, wrl, br, PROMPT_ROWS)

    sr = _project(xs.reshape(bs, d), g_attn, w, bias, gains, hm, bs)
    fq_s, nq_s, fk_s, fv_s, ck_s, cv_s, sk_s, sv_s, wk_s, wv_s, misc_s = sr
    logf_s = misc_s[:, :N_FOX_HEADS]
    gates_s = misc_s[:, N_FOX_HEADS:N_FOX_HEADS + 3 * N_NSA_HEADS].reshape(bs, N_NSA_HEADS, 3)
    q_heads = fq_s.reshape(bs, N_FOX_HEADS, 2, HEAD_DIM)
    q_heads = jnp.stack([q_heads[:, h, h % 2] for h in range(N_FOX_HEADS)], axis=1)
    nq_heads = nq_s.reshape(bs, N_NSA_HEADS, N_NSA_KV, HEAD_DIM)
    in_group = (jnp.arange(N_NSA_HEADS)[:, None] // NSA_GROUP == jnp.arange(N_NSA_KV)[None, :])[None, :, :, None]
    qg = jnp.where(in_group, nq_heads, jnp.zeros_like(nq_heads)).reshape(bs, N_NSA_HEADS, LANES)
    nq_heads = jnp.stack([nq_heads[:, h, h // NSA_GROUP] for h in range(N_NSA_HEADS)], axis=1)
    lane_rep = lambda a: jnp.broadcast_to(a.astype(F32).reshape(bs, -1, 1), (bs, a.size // bs, LANES))
    pages_t = lambda cache: jnp.transpose(cache, (0, 2, 3, 1)).reshape(n_pool, -1, PAGE_SIZE)
    logf_past = c_fox_logf[page_table].reshape(bs, past, N_FOX_HEADS).astype(F32)
    ct_s = _cumsum_lanes(jnp.transpose(logf_past, (0, 2, 1)))
    c_new = jnp.broadcast_to(((ct_s[:, :, -1] + logf_s) * LOG2E)[:, :, None], (bs, N_FOX_HEADS, LANES))
    row3 = lambda a: a.reshape(bs, 1, -1)
    o_fox_s = _fox_sample(page_table, lane_rep(q_heads), lane_rep(fk_s), lane_rep(fv_s), pages_t(c_fox_k),
                          pages_t(c_fox_v), ct_s * LOG2E, c_new, PAGES_PER_STEP)
    wide = CMP_STRIDE * LANES
    oc_s, selh = _cmp_sample(page_table, qg, c_cmp_k.reshape(n_pool, -1, wide), c_cmp_v.reshape(n_pool, -1, wide),
                             w4, pe4, g_cmp2, hm, PAGES_PER_STEP)
    o_sel_s = _sel_sample(page_table, lane_rep(nq_heads), lane_rep(sk_s), lane_rep(sv_s), pages_t(c_sel_k),
                          pages_t(c_sel_v), selh, PAGES_PER_STEP)
    o_sel_s = o_sel_s[:, :, 0].reshape(bs, N_NSA_HEADS, 1, HEAD_DIM)
    o_sel_s = jnp.where(in_group, o_sel_s, 0.0).reshape(bs, N_NSA_HEADS, LANES)
    nbuf = st_win_k.shape[1]
    o_nsa_s = _gate_sample(qg, o_sel_s, st_win_k.reshape(bs, nbuf, -1), st_win_v.reshape(bs, nbuf, -1),
                           row3(wk_s), row3(wv_s), oc_s, gates_s)
    mf_s = o_fox_s[:, :, 0].astype(BF16)
    o_nsa_s = o_nsa_s.reshape(bs, N_NSA_HEADS, N_NSA_KV, HEAD_DIM)
    mn_s = jnp.stack([o_nsa_s[:, h, h // NSA_GROUP] for h in range(N_NSA_HEADS)], axis=1).reshape(bs, -1).astype(BF16)
    x1s, hs, route_s, cnt_s = _tail(xs.reshape(bs, d), mf_s, mn_s, wo, g_ffn, wrh, wrl, br, bs)

    dest_p, dest_s, blk_expert, nused, nblk = _moe_plan(route_p, cnt_p, route_s, cnt_s, MOE_ROWS)
    slots = jnp.zeros((nblk * MOE_ROWS, d // 2), jnp.uint32)
    slots = _dispatch(hp, dest_p, slots, PROMPT_ROWS)
    slots = _dispatch(hs, dest_s, slots, bs)
    ys = _experts(slots, blk_expert, nused, wg, wu, bg, bu, wd, bd, MOE_ROWS)
    yp = _combine(x1p, route_p, dest_p, ys, PROMPT_ROWS).reshape(bp, sp, d)
    y_s = _combine(x1s, route_s, dest_s, ys, bs).reshape(bs, 1, d)

    heads = lambda a, b_, s_: a.reshape(b_, s_, -1, HEAD_DIM)
    wp = min(WINDOW, sp)
    rows_p = (heads(fk, bp, sp), heads(fv, bp, sp), logf, heads(ck, bp, sp), heads(cv, bp, sp), heads(sk, bp, sp),
              heads(sv, bp, sp), heads(wk, bp, sp)[:, -wp:], heads(wv, bp, sp)[:, -wp:])
    win_k_s = jnp.concatenate([st_win_k, heads(wk_s, bs, 1)], axis=1)[:, -nbuf:]
    win_v_s = jnp.concatenate([st_win_v, heads(wv_s, bs, 1)], axis=1)[:, -nbuf:]
    rows_s = (heads(fk_s, bs, 1), heads(fv_s, bs, 1), logf_s.reshape(bs, 1, -1), heads(ck_s, bs, 1),
              heads(cv_s, bs, 1), heads(sk_s, bs, 1), heads(sv_s, bs, 1), win_k_s, win_v_s)
    return yp, y_s, rows_p, rows_s


def kernel(x_prompt, x_sample, cache_fox_k, cache_fox_v, cache_fox_logf, cache_cmp_k, cache_cmp_v, cache_sel_k,
           cache_sel_v, state_win_k, state_win_v, page_table, g_attn, w_in, b_fox_f, b_nsa_gate, g_q_fox, g_k_fox,
           g_q_nsa, g_k_sel, g_k_win, g_k_cmp, pe_cmp_k, pe_cmp_v, w_cmp_k, w_cmp_v, w_out, g_ffn, w_router,
           b_router, w_gate_up, b_gate_up, w_down, b_down):
    depth = w_in.shape[0]
    assert depth == 1, "the two groups are threaded through a single layer"
    l = 0
    yp, y_s, rows_p, rows_s = _layer(
        x_prompt, x_sample, cache_fox_k[l], cache_fox_v[l], cache_fox_logf[l], cache_cmp_k[l], cache_cmp_v[l],
        cache_sel_k[l], cache_sel_v[l], state_win_k[l], state_win_v[l], page_table,
        g_attn[l], w_in[l], b_fox_f[l], b_nsa_gate[l], g_q_fox[l], g_k_fox[l], g_q_nsa[l], g_k_sel[l], g_k_win[l],
        g_k_cmp[l], pe_cmp_k[l], pe_cmp_v[l], w_cmp_k[l], w_cmp_v[l], w_out[l], g_ffn[l], w_router[l], b_router[l],
        w_gate_up[l], b_gate_up[l], w_down[l], b_down[l])
    return (yp, y_s) + tuple(r[None] for r in rows_p) + tuple(r[None] for r in rows_s)
```

```python
import functools

import numpy as np
import jax
import jax.numpy as jnp
from jax import lax
from jax.experimental import pallas as pl
from jax.experimental.pallas import tpu as pltpu

F32 = jnp.float32
BF16 = jnp.bfloat16

HEAD_DIM = 64
LANES = 128
N_FOX_HEADS = 8
N_NSA_HEADS = 8
N_NSA_KV = 2
NSA_GROUP = N_NSA_HEADS // N_NSA_KV
CMP_BLOCK = 32
CMP_STRIDE = 16
SEL_BLOCK = 64
N_SELECT = 16
WINDOW = 512
FORCE_SCORE = 1.0e4
N_EXPERTS = 32
TOP_K = 4
SWIGLU_LIMIT = 7.0
SWIGLU_ALPHA = 1.702
RMS_EPS = 1e-6
PAGE_SIZE = 128
NEG = -1.0e30
LOG2E = float(np.log2(np.e))
LN2 = float(np.log(2.0))
N_BIAS_LANES = 3
MASK_SCORE = -float(2 ** 17)
TINY = float(np.finfo(np.float32).tiny)
VMEM_LIMIT = 56 * 1024 * 1024


def _cparams(sem):
    return pltpu.CompilerParams(dimension_semantics=sem, vmem_limit_bytes=VMEM_LIMIT)


def _dot(a, b):
    return jnp.dot(a, b, preferred_element_type=F32)


def _dot_nt(a, b):
    return lax.dot_general(a, b, (((1,), (1,)), ((), ())), preferred_element_type=F32)


def _dot_tn(a, b):
    return lax.dot_general(a, b, (((0,), (0,)), ((), ())), preferred_element_type=F32)


def _split2(x):
    hi = x.astype(BF16)
    lo = (x - hi.astype(F32)).astype(BF16)
    return hi, lo


def _split3(x):
    hi = x.astype(BF16)
    r = x - hi.astype(F32)
    mid = r.astype(BF16)
    lo = (r - mid.astype(F32)).astype(BF16)
    return hi, mid, lo


def _dot_x2(x, m):
    hi, lo = _split2(x)
    return _dot(hi, m) + _dot(lo, m)


def _head_rms(t, gain, hm):
    outs = []
    for c in range(t.shape[1] // LANES):
        tc = t[:, c * LANES:(c + 1) * LANES]
        ms = _dot_x2(tc * tc, hm)
        outs.append(tc * lax.rsqrt(ms + RMS_EPS) * gain)
    return outs[0] if len(outs) == 1 else jnp.concatenate(outs, axis=1)


def _head_mean_matrix():
    i = np.arange(LANES)
    return jnp.asarray((i[:, None] // HEAD_DIM == i[None, :] // HEAD_DIM) / HEAD_DIM, BF16)


_FQ0, _NQ0, _FK0, _FV0, _KV0, _MISC0, _PROJ_COLS = 0, 1024, 2048, 2560, 3072, 3840, 3968


def _proj_weights(w_in, b_fox_f, b_nsa_gate):
    fw = N_FOX_HEADS * HEAD_DIM
    kvw = N_NSA_KV * HEAD_DIM
    o = 0
    fq = w_in[:, o:o + fw]; o += fw
    fk = w_in[:, o:o + fw]; o += fw
    fv = w_in[:, o:o + fw]; o += fw
    ff = w_in[:, o:o + N_FOX_HEADS]; o += N_FOX_HEADS
    nq = w_in[:, o:o + fw]; o += fw
    kv = w_in[:, o:o + 6 * kvw]; o += 6 * kvw
    ng = w_in[:, o:o + 3 * N_NSA_HEADS]
    d = w_in.shape[0]
    z64 = jnp.zeros((d, HEAD_DIM), w_in.dtype)
    fq_x, nq_x = [], []
    for h in range(N_FOX_HEADS):
        qh = fq[:, h * HEAD_DIM:(h + 1) * HEAD_DIM]
        fq_x += [qh, z64] if h % 2 == 0 else [z64, qh]
    for h in range(N_NSA_HEADS):
        qh = nq[:, h * HEAD_DIM:(h + 1) * HEAD_DIM]
        fq_pos = h // NSA_GROUP
        nq_x += [qh, z64] if fq_pos == 0 else [z64, qh]
    misc = jnp.concatenate([ff, ng, jnp.zeros((d, LANES - N_FOX_HEADS - 3 * N_NSA_HEADS), w_in.dtype)], axis=1)
    w = jnp.concatenate(fq_x + nq_x + [fk, fv, kv, misc], axis=1).astype(BF16)
    bias = jnp.concatenate([b_fox_f, b_nsa_gate,
                            jnp.zeros((LANES - N_FOX_HEADS - 3 * N_NSA_HEADS,), F32)]).reshape(1, LANES)
    return w, bias


def _bf16_pieces(x, n):
    out, r = [], float(x)
    for _ in range(n):
        p = float(np.asarray(r, F32).astype(BF16))
        out.append(p)
        r -= p
    return out


def _query_constants():
    fox = np.zeros((N_FOX_HEADS, LANES), np.float32)
    nsa = np.zeros((N_NSA_HEADS, LANES), np.float32)
    for h in range(N_FOX_HEADS):
        spare = HEAD_DIM * (1 - h % 2)
        fox[h, spare:spare + N_BIAS_LANES] = 1.0
    for h in range(N_NSA_HEADS):
        spare = HEAD_DIM * (1 - h // NSA_GROUP)
        for i, piece in enumerate(_bf16_pieces(_slope(h) * LOG2E, N_BIAS_LANES)):
            nsa[h, spare + 2 * i:spare + 2 * i + 2] = piece
    return jnp.asarray(np.stack([fox.reshape(-1), nsa.reshape(-1)]))


def _proj_kernel(x_ref, g_ref, w_ref, bias_ref, gains_ref, hm_ref, qc_ref,
                 fq_ref, nq_ref, fk_ref, fv_ref, ck_ref, cv_ref, sk_ref, sv_ref, wk_ref, wv_ref, misc_ref):
    x = x_ref[...]
    ms = jnp.mean(x * x, axis=-1, keepdims=True)
    xn = (x * lax.rsqrt(ms + RMS_EPS)) * g_ref[...]
    z = _dot(xn.astype(BF16), w_ref[...])
    hm = hm_ref[...]
    scale = HEAD_DIM ** -0.5 * LOG2E
    fq_ref[...] = (_head_rms(z[:, _FQ0:_NQ0], gains_ref[0:1, :], hm) * scale + qc_ref[0:1, :]).astype(BF16)
    nq_ref[...] = (_head_rms(z[:, _NQ0:_FK0], gains_ref[1:2, :], hm) * scale + qc_ref[1:2, :]).astype(BF16)
    fk_ref[...] = _head_rms(z[:, _FK0:_FV0], gains_ref[2:3, :], hm)
    fv_ref[...] = z[:, _FV0:_KV0]
    kv = z[:, _KV0:_MISC0]
    ck_ref[...] = kv[:, 0:128]
    cv_ref[...] = kv[:, 128:256]
    sk_ref[...] = _head_rms(kv[:, 256:384], gains_ref[3:4, :], hm)
    sv_ref[...] = kv[:, 384:512]
    wk_ref[...] = _head_rms(kv[:, 512:640], gains_ref[4:5, :], hm)
    wv_ref[...] = kv[:, 640:768]
    zb = z[:, _MISC0:_PROJ_COLS] + bias_ref[...]
    soft = jnp.log1p(jnp.exp(-jnp.abs(zb)))
    logsig = jnp.minimum(zb, 0.0) - soft
    lane = lax.broadcasted_iota(jnp.int32, zb.shape, 1)
    misc_ref[...] = jnp.where(lane < N_FOX_HEADS, logsig, jnp.exp(logsig))


def _project(x2d, g_attn, w, bias, gains, hm, tm):
    n, d = x2d.shape
    row = lambda c: pl.BlockSpec((tm, c), lambda i: (i, 0))
    full = lambda a: pl.BlockSpec(a.shape, lambda i: (0,) * a.ndim)
    g2 = g_attn.reshape(1, d)
    qc = _query_constants()
    out_shapes = ([jax.ShapeDtypeStruct((n, 1024), BF16)] * 2 + [jax.ShapeDtypeStruct((n, 512), F32)] * 2
                  + [jax.ShapeDtypeStruct((n, LANES), F32)] * 7)
    out_specs = [row(1024), row(1024), row(512), row(512)] + [row(LANES)] * 7
    return pl.pallas_call(
        _proj_kernel, grid=(n // tm,),
        in_specs=[row(d), full(g2), full(w), full(bias), full(gains), full(hm), full(qc)],
        out_specs=out_specs, out_shape=out_shapes,
        compiler_params=_cparams(("parallel",)), name="proj",
    )(x2d, g2, w, bias, gains, hm, qc)


def _dot_x3(x, m):
    hi, mid, lo = _split3(x)
    return _dot(hi, m) + _dot(mid, m) + _dot(lo, m)


def _cumsum_kernel(x_ref, u_ref, sl_ref, o_ref):
    local = _dot_x3(x_ref[0], u_ref[...])
    hi, mid, lo = _split3(jnp.broadcast_to(local[:, LANES - 1:LANES], local.shape))
    sl = sl_ref[...]
    o_ref[0] = local + (_dot(sl, hi) + _dot(sl, mid) + _dot(sl, lo))


def _cumsum_lanes(xt):
    b, h, s = xt.shape
    nchunk = s // LANES
    rows = h * nchunk
    i = np.arange(LANES)
    u = jnp.asarray(i[:, None] <= i[None, :], BF16)
    r = np.arange(rows)
    sl = jnp.asarray((r[:, None] // nchunk == r[None, :] // nchunk) & (r[None, :] < r[:, None]), BF16)
    spec = pl.BlockSpec((1, rows, LANES), lambda i: (i, 0, 0))
    out = pl.pallas_call(
        _cumsum_kernel, grid=(b,),
        in_specs=[spec, pl.BlockSpec((LANES, LANES), lambda i: (0, 0)), pl.BlockSpec((rows, rows), lambda i: (0, 0))],
        out_specs=spec, out_shape=jax.ShapeDtypeStruct((b, rows, LANES), F32),
        compiler_params=_cparams(("parallel",)), name="cumsum",
    )(xt.reshape(b, rows, LANES), u, sl)
    return out.reshape(b, h, s)


def _tri_pairs(n):
    qi = np.concatenate([np.full(i + 1, i) for i in range(n)]).astype(np.int32)
    ki = np.concatenate([np.arange(i + 1) for i in range(n)]).astype(np.int32)
    return jnp.asarray(qi), jnp.asarray(ki)


def _softmax_step(s, v, m_ref, l_ref, acc_ref, idx):
    m_prev = m_ref[idx]
    m_new = jnp.maximum(m_prev, jnp.max(s, axis=1, keepdims=True))
    alpha = jnp.exp(m_prev - m_new)
    p = jnp.exp(s - m_new)
    l_ref[idx] = alpha * l_ref[idx] + jnp.sum(p, axis=1, keepdims=True)
    acc_ref[idx] = alpha * acc_ref[idx] + _dot(p.astype(BF16), v)
    m_ref[idx] = m_new


def _init_softmax(m_ref, l_ref, acc_ref):
    m_ref[...] = jnp.full(m_ref.shape, NEG, F32)
    l_ref[...] = jnp.zeros(l_ref.shape, F32)
    acc_ref[...] = jnp.zeros(acc_ref.shape, F32)


def _exp2_step(s, v, m_ref, l_ref, acc_ref, idx):
    cols = s.shape[1] // LANES
    m_prev = m_ref[idx]
    m_new = jnp.maximum(m_prev, jnp.max(s, axis=1, keepdims=True))
    alpha = jnp.exp2(m_prev - m_new)
    p = jnp.exp2(s - jnp.tile(m_new, (1, cols)))
    l_ref[idx] = alpha * l_ref[idx] + sum(p[:, c * LANES:(c + 1) * LANES] for c in range(cols))
    acc_ref[idx] = alpha * acc_ref[idx] + _dot(p.astype(BF16), v)
    m_ref[idx] = m_new


def _softmax_result(l_ref, acc_ref, idx):
    return acc_ref[idx] / jnp.sum(l_ref[idx], axis=1, keepdims=True)


def _lower_tri(t):
    return lax.broadcasted_iota(jnp.int32, (t, t), 1) <= lax.broadcasted_iota(jnp.int32, (t, t), 0)


def _fox_bias_kernel(x_ref, lt_ref, sl_ref, pm_ref, a_ref, c_ref, tot_ref):
    nchunk = tot_ref.shape[0]
    lt = lt_ref[...]

    def local(i, carry):
        rows = pl.ds(pl.multiple_of(i * LANES, LANES), LANES)
        hi, mid, lo = _split3(x_ref[0, rows, :])
        c = _dot(lt, hi) + _dot(lt, mid) + _dot(lt, lo)
        c_ref[rows, :] = c
        tot_ref[pl.ds(i, 1), :] = c[LANES - 1:LANES, :]
        return carry

    lax.fori_loop(0, nchunk, local, 0)
    hi, mid, lo = _split3(tot_ref[...])
    sl = sl_ref[...]
    tot_ref[...] = _dot(sl, hi) + _dot(sl, mid) + _dot(sl, lo)

    def place(i, carry):
        rows = pl.ds(pl.multiple_of(i * LANES, LANES), LANES)
        hi, mid, lo = _split3((c_ref[rows, :] + tot_ref[pl.ds(i, 1), :]) * (-LOG2E))
        for hp in range(a_ref.shape[1]):
            a = _dot(hi, pm_ref[hp, 0]) + _dot(mid, pm_ref[hp, 1]) + _dot(lo, pm_ref[hp, 2])
            a_ref[0, hp, rows, :] = a.astype(BF16)
        return carry

    lax.fori_loop(0, nchunk, place, 0)


def _fox_bias(misc):
    b, s, _ = misc.shape
    nchunk = s // LANES
    i = np.arange(LANES)
    lt = jnp.asarray(i[:, None] >= i[None, :], BF16)
    c = np.arange(nchunk)
    sl = jnp.asarray(c[:, None] > c[None, :], BF16)
    pm = np.zeros((N_FOX_HEADS // 2, N_BIAS_LANES, LANES, LANES), np.float32)
    for hp in range(N_FOX_HEADS // 2):
        for piece in range(N_BIAS_LANES):
            pm[hp, piece, 2 * hp, HEAD_DIM + piece] = 1.0
            pm[hp, piece, 2 * hp + 1, piece] = 1.0
    pm = jnp.asarray(pm, BF16)
    full = lambda a: pl.BlockSpec(a.shape, lambda i: (0,) * a.ndim)
    return pl.pallas_call(
        _fox_bias_kernel, grid=(b,),
        in_specs=[pl.BlockSpec((1, s, LANES), lambda i: (i, 0, 0)), full(lt), full(sl), full(pm)],
        out_specs=pl.BlockSpec((1, N_FOX_HEADS // 2, s, LANES), lambda i: (i, 0, 0, 0)),
        out_shape=jax.ShapeDtypeStruct((b, N_FOX_HEADS // 2, s, LANES), BF16),
        scratch_shapes=[pltpu.VMEM((s, LANES), F32), pltpu.VMEM((nchunk, LANES), F32)],
        compiler_params=_cparams(("parallel",)), name="fox_bias",
    )(misc, lt, sl, pm)


def _fox_kernel(qt_ref, kt_ref, q_ref, k_ref, v_ref, a_ref, o_ref, m_ref, l_ref, acc_ref, *, t):
    step = pl.program_id(2)
    qi = qt_ref[step]
    ki = kt_ref[step]

    @pl.when(ki == 0)
    def _():
        _init_softmax(m_ref, l_ref, acc_ref)

    def update(diag):
        k = k_ref[0].astype(BF16)
        v = v_ref[0].astype(BF16)
        a = a_ref[0, 0]
        lane = lax.broadcasted_iota(jnp.int32, k.shape, 1)
        keys = (jnp.where(lane < HEAD_DIM, k, a), jnp.where(lane < HEAD_DIM, a, k))
        for h in range(2):
            s = _dot_nt(q_ref[0, :, h * LANES:(h + 1) * LANES], keys[h])
            if diag:
                s = jnp.where(_lower_tri(t), s, NEG)
            _exp2_step(s, v, m_ref, l_ref, acc_ref, h)

    @pl.when(ki < qi)
    def _():
        update(False)

    @pl.when(ki == qi)
    def _():
        update(True)
        lane = lax.broadcasted_iota(jnp.int32, (t, LANES), 1)
        o = jnp.where(lane < HEAD_DIM, _softmax_result(l_ref, acc_ref, 0), _softmax_result(l_ref, acc_ref, 1))
        o_ref[0] = o.astype(BF16)


def _fox_prompt(fq_x, fk, fv, bias, t):
    b, s, _ = fk.shape
    qt, kt = _tri_pairs(s // t)
    grid_spec = pltpu.PrefetchScalarGridSpec(
        num_scalar_prefetch=2, grid=(b, N_FOX_HEADS // 2, qt.shape[0]),
        in_specs=[
            pl.BlockSpec((1, t, 2 * LANES), lambda b, hp, i, qt, kt: (b, qt[i], hp)),
            pl.BlockSpec((1, t, LANES), lambda b, hp, i, qt, kt: (b, kt[i], hp)),
            pl.BlockSpec((1, t, LANES), lambda b, hp, i, qt, kt: (b, kt[i], hp)),
            pl.BlockSpec((1, 1, t, LANES), lambda b, hp, i, qt, kt: (b, hp, kt[i], 0)),
        ],
        out_specs=pl.BlockSpec((1, t, LANES), lambda b, hp, i, qt, kt: (b, qt[i], hp)),
        scratch_shapes=[pltpu.VMEM((2, t, LANES), F32)] * 3,
    )
    return pl.pallas_call(
        functools.partial(_fox_kernel, t=t), grid_spec=grid_spec,
        out_shape=jax.ShapeDtypeStruct((b, s, 512), BF16),
        compiler_params=_cparams(("parallel", "parallel", "arbitrary")), name="fox_prompt",
    )(qt, kt, fq_x, fk, fv, bias)


def _cmp_weights(w_cmp, pe):
    wl = w_cmp.reshape(CMP_BLOCK, HEAD_DIM, HEAD_DIM)
    eye = jnp.eye(N_NSA_KV, dtype=w_cmp.dtype)
    half = CMP_BLOCK // 2
    w2 = jnp.einsum('lde,gh->lgdhe', wl, eye).reshape(2, half * N_NSA_KV * HEAD_DIM, N_NSA_KV * HEAD_DIM)
    pe2 = jnp.tile(pe.reshape(2, half, 1, HEAD_DIM), (1, 1, N_NSA_KV, 1)).reshape(2, half * N_NSA_KV * HEAD_DIM)
    return w2.astype(BF16), pe2


def _compress(x, w_ref, pe_ref, i0):
    n = x.shape[0]
    a = _dot((x + pe_ref[i0:i0 + 1, :]).astype(BF16), w_ref[i0])
    b = _dot((x + pe_ref[i0 + 1:i0 + 2, :]).astype(BF16), w_ref[i0 + 1])
    return a + pltpu.roll(b, n - 1, axis=0)


def _cmp_kernel(xk_ref, xv_ref, w_ref, pe_ref, g_ref, hm_ref, kc_ref, vc_ref):
    kc = _compress(xk_ref[0], w_ref, pe_ref, 0)
    kc_ref[0] = _head_rms(kc, g_ref[...], hm_ref[...]).astype(BF16)
    vc_ref[0] = _compress(xv_ref[0], w_ref, pe_ref, 2).astype(BF16)


def _cmp_prompt(ck, cv, w4, pe4, g_k_cmp2, hm):
    b, s, _ = ck.shape
    n = s // CMP_STRIDE
    wide = CMP_STRIDE * LANES
    xspec = pl.BlockSpec((1, n, wide), lambda i: (i, 0, 0))
    full = lambda a: pl.BlockSpec(a.shape, lambda i: (0,) * a.ndim)
    ospec = pl.BlockSpec((1, n, LANES), lambda i: (i, 0, 0))
    return pl.pallas_call(
        _cmp_kernel, grid=(b,),
        in_specs=[xspec, xspec, full(w4), full(pe4), full(g_k_cmp2), full(hm)],
        out_specs=[ospec, ospec], out_shape=[jax.ShapeDtypeStruct((b, n, LANES), BF16)] * 2,
        compiler_params=_cparams(("parallel",)), name="cmp_prompt",
    )(ck.reshape(b, n, wide), cv.reshape(b, n, wide), w4, pe4, g_k_cmp2, hm)


def _slope(h):
    return float(2.0 ** (-8.0 * (h + 1) / N_NSA_HEADS))


def _overlap_matrix(nc, nsb):
    c0 = np.arange(nc)[:, None] * CMP_STRIDE
    s0 = np.arange(nsb)[None, :] * SEL_BLOCK
    return jnp.asarray((c0 < s0 + SEL_BLOCK) & (c0 + CMP_BLOCK > s0), BF16)


def _select_blocks(imp, t, nsb):
    jb = lax.broadcasted_iota(jnp.int32, imp.shape, 1)
    cur = t // SEL_BLOCK
    visible = jb * SEL_BLOCK <= t
    work = jnp.where((jb == 0) | (jb == cur) | (jb == cur - 1), FORCE_SCORE, imp)
    work = jnp.where(visible, work, NEG)
    jbf = jb.astype(F32)
    sel = jnp.zeros(imp.shape, F32)
    for _ in range(min(N_SELECT, nsb)):
        mx = jnp.max(work, axis=1, keepdims=True)
        first = jnp.min(jnp.where(work == mx, jbf, float(nsb)), axis=1, keepdims=True)
        pick = jbf == first
        sel = jnp.where(pick, 1.0, sel)
        work = jnp.where(pick, 2.0 * NEG, work)
    return jnp.where(visible, sel, 0.0)


def _cmp_attend(q, kc, vc, d, slope):
    valid = d >= 0.0
    s = jnp.where(valid, _dot_nt(q, kc) * LN2 - slope * d, NEG)
    m = jnp.max(s, axis=1, keepdims=True)
    e = jnp.where(valid, jnp.exp(s - m), 0.0)
    p = e / jnp.maximum(jnp.sum(e, axis=1, keepdims=True), TINY)
    return p, _dot(p.astype(BF16), vc)


def _data_lanes(q, g):
    lane = lax.broadcasted_iota(jnp.int32, q.shape, 1)
    return jnp.where(lane // HEAD_DIM == g, q, jnp.zeros_like(q))


def _nsa_cmp_kernel(q_ref, kc_ref, vc_ref, ov_ref, oc_ref, unsel_ref, *, tq, nc_valid):
    qi = pl.program_id(1)
    ncp = kc_ref.shape[1]
    nsbp = ov_ref.shape[1]
    t = qi * tq + lax.broadcasted_iota(jnp.int32, (tq, 1), 0)
    n = lax.broadcasted_iota(jnp.int32, (1, ncp), 1)
    cend = jnp.where(n < nc_valid, n * CMP_STRIDE + CMP_BLOCK - 1, jnp.int32(2 ** 30))
    d = (t - cend).astype(F32)
    kc = kc_ref[0]
    vc = vc_ref[0]
    ov = ov_ref[...]
    for g in range(N_NSA_KV):
        psum = jnp.zeros((tq, ncp), F32)
        for j in range(NSA_GROUP):
            h = g * NSA_GROUP + j
            p, o = _cmp_attend(_data_lanes(q_ref[0, :, h * LANES:(h + 1) * LANES], g), kc, vc, d, _slope(h))
            oc_ref[0, :, h * LANES:(h + 1) * LANES] = o
            psum = psum + p
        unsel_ref[0, g] = (1.0 - _select_blocks(_dot_x2(psum, ov), t, nsbp)).astype(BF16)


def _nsa_cmp_prompt(nq_x, kc, vc, tq):
    b, s, _ = nq_x.shape
    ncp = kc.shape[1]
    nc_valid = (s - CMP_BLOCK) // CMP_STRIDE + 1
    nsb = -(-s // SEL_BLOCK)
    assert nsb <= LANES, "the block map rides in one 128-lane contraction tile"
    ov = _overlap_matrix(ncp, LANES)
    return pl.pallas_call(
        functools.partial(_nsa_cmp_kernel, tq=tq, nc_valid=nc_valid), grid=(b, s // tq),
        in_specs=[pl.BlockSpec((1, tq, 1024), lambda b, i: (b, i, 0)),
                  pl.BlockSpec((1, ncp, LANES), lambda b, i: (b, 0, 0)),
                  pl.BlockSpec((1, ncp, LANES), lambda b, i: (b, 0, 0)),
                  pl.BlockSpec((ncp, LANES), lambda b, i: (0, 0))],
        out_specs=[pl.BlockSpec((1, tq, 1024), lambda b, i: (b, i, 0)),
                   pl.BlockSpec((1, N_NSA_KV, tq, LANES), lambda b, i: (b, 0, i, 0))],
        out_shape=[jax.ShapeDtypeStruct((b, s, 1024), F32), jax.ShapeDtypeStruct((b, N_NSA_KV, s, LANES), BF16)],
        compiler_params=_cparams(("parallel", "parallel")), name="nsa_cmp_prompt",
    )(nq_x, kc, vc, ov)


def _pack_heads(o_a, o_b, g):
    lane = lax.broadcasted_iota(jnp.int32, o_a.shape, 1)
    first = jnp.where(lane < HEAD_DIM, o_a, pltpu.roll(o_b, HEAD_DIM, axis=1))
    second = jnp.where(lane < HEAD_DIM, pltpu.roll(o_a, HEAD_DIM, axis=1), o_b)
    return jnp.where(g == 0, first, second)


def _nsa_mix_kernel(qt_ref, kt_ref, q_ref, sk_ref, sv_ref, unsel_ref, wkp_ref, wvp_ref, wkc_ref, wvc_ref,
                    oc_ref, misc_ref, o_ref, m_ref, l_ref, acc_ref, *, t):
    g = pl.program_id(1)
    step = pl.program_id(2)
    qi = qt_ref[step]
    ki = kt_ref[step]

    @pl.when(ki == 0)
    def _():
        _init_softmax(m_ref, l_ref, acc_ref)

    def update(diag):
        k = sk_ref[0].astype(BF16)
        v = sv_ref[0].astype(BF16)
        lane = lax.broadcasted_iota(jnp.int32, (t, LANES), 1)
        row = lax.broadcasted_iota(jnp.int32, (t, LANES), 0)
        pos = ki * t + row
        spare = lane - HEAD_DIM * (1 - g)
        pos_lo = pos & 255
        pos_part = jnp.where((spare & 1) == 0, pos - pos_lo, pos_lo).astype(F32).astype(BF16)
        keys = jnp.where((spare >= 0) & (spare < 2 * N_BIAS_LANES), pos_part, k)
        blk = ki * (t // SEL_BLOCK) + row // SEL_BLOCK
        rhs = jnp.concatenate([keys, jnp.where(lane == blk, MASK_SCORE, 0.0).astype(BF16)], axis=1)
        unsel = unsel_ref[0, 0]
        for j in range(NSA_GROUP):
            lhs = jnp.concatenate([q_ref[0, :, j * LANES:(j + 1) * LANES], unsel], axis=1)
            s = _dot_nt(lhs, rhs)
            if diag:
                s = jnp.where(_lower_tri(t), s, NEG)
            _exp2_step(s, v, m_ref, l_ref, acc_ref, j)

    @pl.when(ki < qi)
    def _():
        update(False)

    @pl.when(ki == qi)
    def _():
        update(True)
        kw = jnp.concatenate([wkp_ref[0], wkc_ref[0]], axis=0).astype(BF16)
        vw = jnp.concatenate([wvp_ref[0], wvc_ref[0]], axis=0).astype(BF16)
        nw = 2 * WINDOW
        dw = WINDOW + lax.broadcasted_iota(jnp.int32, (WINDOW, nw), 0) - lax.broadcasted_iota(jnp.int32, (WINDOW, nw), 1)
        band = (dw >= 0) & (dw <= WINDOW)
        dwf = dw.astype(F32)
        gates = misc_ref[0]
        outs = []
        for j in range(NSA_GROUP):
            slope = jnp.where(g == 0, _slope(j), _slope(NSA_GROUP + j))
            q = _data_lanes(q_ref[0, :, j * LANES:(j + 1) * LANES], g)
            strips = []
            for r in range(t // WINDOW):
                kp = qi * t + (r - 1) * WINDOW + lax.broadcasted_iota(jnp.int32, (1, nw), 1)
                keys = slice(r * WINDOW, r * WINDOW + nw)
                s = _dot_nt(q[r * WINDOW:(r + 1) * WINDOW], kw[keys]) * LN2 - slope * dwf
                s = jnp.where(band & (kp >= 0), s, NEG)
                e = jnp.exp(s - jnp.max(s, axis=1, keepdims=True))
                strips.append(_dot(e.astype(BF16), vw[keys]) / jnp.sum(e, axis=1, keepdims=True))
            o_w = jnp.concatenate(strips, axis=0)
            c0 = N_FOX_HEADS + 3 * j
            c1 = c0 + 3 * NSA_GROUP
            gate = [jnp.where(g == 0, gates[:, c0 + c:c0 + c + 1], gates[:, c1 + c:c1 + c + 1]) for c in range(3)]
            outs.append(gate[0] * oc_ref[0, :, j * LANES:(j + 1) * LANES]
                        + gate[1] * _softmax_result(l_ref, acc_ref, j) + gate[2] * o_w)
        for c in range(NSA_GROUP // 2):
            o_ref[0, :, c * LANES:(c + 1) * LANES] = _pack_heads(outs[2 * c], outs[2 * c + 1], g).astype(BF16)


def _nsa_mix_prompt(nq_x, sk, sv, unsel, wk, wv, oc_x, misc, t):
    b, s, _ = nq_x.shape
    assert t % WINDOW == 0
    qt, kt = _tri_pairs(s // t)
    gw = NSA_GROUP * LANES
    qmap = lambda b, g, i, qt, kt: (b, qt[i], 0)
    gmap = lambda b, g, i, qt, kt: (b, qt[i], g)
    kmap = lambda b, g, i, qt, kt: (b, kt[i], 0)
    pmap = lambda b, g, i, qt, kt: (b, jnp.maximum(qt[i] * (t // WINDOW) - 1, 0), 0)
    grid_spec = pltpu.PrefetchScalarGridSpec(
        num_scalar_prefetch=2, grid=(b, N_NSA_KV, qt.shape[0]),
        in_specs=[
            pl.BlockSpec((1, t, gw), gmap),
            pl.BlockSpec((1, t, LANES), kmap), pl.BlockSpec((1, t, LANES), kmap),
            pl.BlockSpec((1, 1, t, LANES), lambda b, g, i, qt, kt: (b, g, qt[i], 0)),
            pl.BlockSpec((1, WINDOW, LANES), pmap), pl.BlockSpec((1, WINDOW, LANES), pmap),
            pl.BlockSpec((1, t, LANES), qmap), pl.BlockSpec((1, t, LANES), qmap),
            pl.BlockSpec((1, t, gw), gmap),
            pl.BlockSpec((1, t, LANES), qmap),
        ],
        out_specs=pl.BlockSpec((1, t, gw // 2), gmap),
        scratch_shapes=[pltpu.VMEM((NSA_GROUP, t, LANES), F32)] * 3,
    )
    return pl.pallas_call(
        functools.partial(_nsa_mix_kernel, t=t), grid_spec=grid_spec,
        out_shape=jax.ShapeDtypeStruct((b, s, 512), BF16),
        compiler_params=_cparams(("parallel", "parallel", "arbitrary")), name="nsa_mix_prompt",
    )(qt, kt, nq_x, sk, sv, unsel, wk, wv, wk, wv, oc_x, misc)


_ROUTE_E, _ROUTE_GATE, _ROUTE_RANK = 0, TOP_K, 2 * TOP_K


def _router_weights(w_router, b_router):
    d = w_router.shape[0]
    wp = jnp.concatenate([w_router, jnp.zeros((d, LANES - N_EXPERTS), F32)], axis=1)
    hi = wp.astype(BF16)
    lo = (wp - hi.astype(F32)).astype(BF16)
    bias = jnp.concatenate([b_router.astype(F32), jnp.full((LANES - N_EXPERTS,), NEG, F32)]).reshape(1, LANES)
    return hi, lo, bias


def _tail_kernel(x_ref, mf_ref, mn_ref, wo_ref, g_ref, wrh_ref, wrl_ref, br_ref, tri_ref,
                 x1_ref, h_ref, route_ref, cnt_ref):
    @pl.when(pl.program_id(0) == 0)
    def _():
        cnt_ref[...] = jnp.zeros(cnt_ref.shape, F32)

    half = mf_ref.shape[1]
    x1 = x_ref[...] + _dot(mf_ref[...], wo_ref[0:half, :]) + _dot(mn_ref[...], wo_ref[half:2 * half, :])
    x1_ref[...] = x1
    h = (x1 * lax.rsqrt(jnp.mean(x1 * x1, axis=-1, keepdims=True) + RMS_EPS)) * g_ref[...]
    h_hi, h_lo = _split2(h)
    bits = lax.bitcast_convert_type(h_hi.astype(F32), jnp.uint32)
    words = bits.shape[1] // 2
    h_ref[...] = bits[:, :words] | (bits[:, words:] >> 16)
    logits = _dot(h_hi, wrh_ref[...]) + _dot(h_lo, wrh_ref[...]) + _dot(h_hi, wrl_ref[...]) + br_ref[...]
    lane = lax.broadcasted_iota(jnp.int32, logits.shape, 1)
    lanef = lane.astype(F32)
    work = logits
    picks, vals, ids = [], [], []
    for _ in range(TOP_K):
        mx = jnp.max(work, axis=1, keepdims=True)
        first = jnp.min(jnp.where(work == mx, lanef, float(LANES)), axis=1, keepdims=True)
        pick = lanef == first
        work = jnp.where(pick, 2.0 * NEG, work)
        picks.append(pick); vals.append(mx); ids.append(first)
    ex = [jnp.exp(v - vals[0]) for v in vals]
    den = ex[0] + ex[1] + ex[2] + ex[3]
    onehot = jnp.zeros(logits.shape, F32)
    for p in picks:
        onehot = jnp.where(p, 1.0, onehot)
    carry = cnt_ref[0:1, :]
    ranks = _dot(tri_ref[...], onehot.astype(BF16)) + carry
    route = jnp.zeros(logits.shape, F32)
    for k in range(TOP_K):
        rank_k = jnp.sum(jnp.where(picks[k], ranks, 0.0), axis=1, keepdims=True)
        route = jnp.where(lane == _ROUTE_E + k, ids[k], route)
        route = jnp.where(lane == _ROUTE_GATE + k, ex[k] / den, route)
        route = jnp.where(lane == _ROUTE_RANK + k, rank_k, route)
    route_ref[...] = route
    cnt_ref[...] = jnp.broadcast_to(carry + jnp.sum(onehot, axis=0, keepdims=True), cnt_ref.shape)


def _tail(x2d, mf, mn, wo, g_ffn, wrh, wrl, br, tm):
    n, d = x2d.shape
    i = np.arange(tm)
    tri = jnp.asarray(i[:, None] > i[None, :], BF16)
    row = lambda c: pl.BlockSpec((tm, c), lambda i: (i, 0))
    full = lambda a: pl.BlockSpec(a.shape, lambda i: (0,) * a.ndim)
    g2 = g_ffn.reshape(1, d)
    return pl.pallas_call(
        _tail_kernel, grid=(n // tm,),
        in_specs=[row(d), row(mf.shape[1]), row(mn.shape[1]), full(wo), full(g2), full(wrh), full(wrl), full(br),
                  full(tri)],
        out_specs=[row(d), row(d // 2), row(LANES), pl.BlockSpec((8, LANES), lambda i: (0, 0))],
        out_shape=[jax.ShapeDtypeStruct((n, d), F32), jax.ShapeDtypeStruct((n, d // 2), jnp.uint32),
                   jax.ShapeDtypeStruct((n, LANES), F32), jax.ShapeDtypeStruct((8, LANES), F32)],
        compiler_params=_cparams(("arbitrary",)), name="tail",
    )(x2d, mf, mn, wo, g2, wrh, wrl, br, tri)


def _row_copy(src_hbm, src_row, dst, dst_row, sem):
    return pltpu.make_async_copy(src_hbm.at[pl.ds(src_row, 1)], dst.at[pl.ds(dst_row, 1)], sem)


def _dispatch_kernel(dest_ref, h_ref, xs_in_hbm, xs_hbm, sem, *, tm):
    del xs_in_hbm

    def issue(r, c):
        for k in range(TOP_K):
            _row_copy(h_ref, r, xs_hbm, dest_ref[0, 0, TOP_K * r + k], sem).start()
        return c

    def drain(r, c):
        for k in range(TOP_K):
            _row_copy(h_ref, r, xs_hbm, dest_ref[0, 0, TOP_K * r + k], sem).wait()
        return c

    lax.fori_loop(0, tm, issue, 0)
    lax.fori_loop(0, tm, drain, 0)


def _dispatch(h, dest, xs, tm):
    n, d = h.shape
    dest3 = dest.reshape(n // tm, 1, tm * TOP_K)
    return pl.pallas_call(
        functools.partial(_dispatch_kernel, tm=tm), grid=(n // tm,),
        in_specs=[pl.BlockSpec((1, 1, tm * TOP_K), lambda i: (i, 0, 0), memory_space=pltpu.SMEM),
                  pl.BlockSpec((tm, d), lambda i: (i, 0)), pl.BlockSpec(memory_space=pl.ANY)],
        out_specs=pl.BlockSpec(memory_space=pl.ANY),
        out_shape=jax.ShapeDtypeStruct(xs.shape, xs.dtype),
        scratch_shapes=[pltpu.SemaphoreType.DMA(())],
        input_output_aliases={2: 0},
        compiler_params=_cparams(("arbitrary",)), name="moe_dispatch",
    )(dest3, h, xs)


_PAIR = 2 * LANES


def _split_gate_up_kernel(w_ref, p_ref, g_ref, u_ref):
    p = p_ref[...]
    for c in range(w_ref.shape[2] // _PAIR):
        t = _dot(w_ref[0, :, c * _PAIR:(c + 1) * _PAIR].astype(BF16), p)
        g_ref[0, :, c * LANES:(c + 1) * LANES] = t[:, :LANES].astype(BF16)
        u_ref[0, :, c * LANES:(c + 1) * LANES] = t[:, LANES:].astype(BF16)


def _split_gate_up(w_gate_up, rows):
    e, d, de2 = w_gate_up.shape
    perm = np.zeros((_PAIR, _PAIR), np.float32)
    j = np.arange(LANES)
    perm[2 * j, j] = 1.0
    perm[2 * j + 1, LANES + j] = 1.0
    ospec = pl.BlockSpec((1, rows, de2 // 2), lambda i, r: (i, r, 0))
    return pl.pallas_call(
        _split_gate_up_kernel, grid=(e, d // rows),
        in_specs=[pl.BlockSpec((1, rows, de2), lambda i, r: (i, r, 0)),
                  pl.BlockSpec((_PAIR, _PAIR), lambda i, r: (0, 0))],
        out_specs=[ospec, ospec], out_shape=[jax.ShapeDtypeStruct((e, d, de2 // 2), BF16)] * 2,
        compiler_params=_cparams(("parallel", "parallel")), name="split_gate_up",
    )(w_gate_up, jnp.asarray(perm, BF16))


def _expert_kernel(be_ref, nused_ref, x_ref, wg_ref, wu_ref, bg_ref, bu_ref, wd_ref, bd_ref, y_ref):
    @pl.when(pl.program_id(0) < nused_ref[0])
    def _():
        words = x_ref[...]
        left = lax.bitcast_convert_type(words & jnp.uint32(0xFFFF0000), F32)
        right = lax.bitcast_convert_type(words << 16, F32)
        x = jnp.concatenate([left, right], axis=1).astype(BF16)
        g = jnp.minimum(_dot(x, wg_ref[0]) + bg_ref[0], SWIGLU_LIMIT)
        u = jnp.clip(_dot(x, wu_ref[0]) + bu_ref[0], -SWIGLU_LIMIT, SWIGLU_LIMIT)
        act = (u + 1.0) * g * jax.nn.sigmoid(SWIGLU_ALPHA * g)
        y_ref[...] = _dot(act.astype(BF16), wd_ref[0]) + bd_ref[0]

    @pl.when(pl.program_id(0) >= nused_ref[0])
    def _():
        y_ref[...] = jnp.zeros(y_ref.shape, F32)


def _experts(xs, blk_expert, nused, wg, wu, bg, bu, wd, bd, mb):
    n_slots, words = xs.shape
    _, d, de = wg.shape
    rows = lambda i, be, nu: (jnp.minimum(i, nu[0] - 1), 0)
    wmap = lambda i, be, nu: (be[i], 0, 0)
    grid_spec = pltpu.PrefetchScalarGridSpec(
        num_scalar_prefetch=2, grid=(n_slots // mb,),
        in_specs=[pl.BlockSpec((mb, words), rows),
                  pl.BlockSpec((1, d, de), wmap), pl.BlockSpec((1, d, de), wmap),
                  pl.BlockSpec((1, 1, de), wmap), pl.BlockSpec((1, 1, de), wmap),
                  pl.BlockSpec((1, de, d), wmap), pl.BlockSpec((1, 1, d), wmap)],
        out_specs=pl.BlockSpec((mb, d), lambda i, be, nu: (i, 0)),
    )
    return pl.pallas_call(
        _expert_kernel, grid_spec=grid_spec, out_shape=jax.ShapeDtypeStruct((n_slots, d), F32),
        compiler_params=_cparams(("arbitrary",)), name="moe_experts",
    )(blk_expert, nused, xs, wg, wu, bg, bu, wd, bd)


def _combine_kernel(dest_ref, route_ref, x1_ref, ys_hbm, o_ref, buf_ref, sem, *, tm):
    def issue(r, c):
        for k in range(TOP_K):
            _row_copy(ys_hbm, dest_ref[0, 0, TOP_K * r + k], buf_ref.at[k], r, sem).start()
        return c

    def drain(r, c):
        for k in range(TOP_K):
            _row_copy(ys_hbm, dest_ref[0, 0, TOP_K * r + k], buf_ref.at[k], r, sem).wait()
        return c

    lax.fori_loop(0, tm, issue, 0)
    lax.fori_loop(0, tm, drain, 0)
    route = route_ref[...]
    y = route[:, _ROUTE_GATE:_ROUTE_GATE + 1] * buf_ref[0]
    for k in range(1, TOP_K):
        y = y + route[:, _ROUTE_GATE + k:_ROUTE_GATE + k + 1] * buf_ref[k]
    o_ref[...] = x1_ref[...] + y


def _combine(x1, route, dest, ys, tm):
    n, d = x1.shape
    dest3 = dest.reshape(n // tm, 1, tm * TOP_K)
    row = lambda c: pl.BlockSpec((tm, c), lambda i: (i, 0))
    return pl.pallas_call(
        functools.partial(_combine_kernel, tm=tm), grid=(n // tm,),
        in_specs=[pl.BlockSpec((1, 1, tm * TOP_K), lambda i: (i, 0, 0), memory_space=pltpu.SMEM),
                  row(LANES), row(d), pl.BlockSpec(memory_space=pl.ANY)],
        out_specs=row(d), out_shape=jax.ShapeDtypeStruct((n, d), F32),
        scratch_shapes=[pltpu.VMEM((TOP_K, tm, d), F32), pltpu.SemaphoreType.DMA(())],
        compiler_params=_cparams(("arbitrary",)), name="moe_combine",
    )(dest3, route, x1, ys)


def _moe_plan(route_p, cnt_p, route_s, cnt_s, mb):
    n_tok = route_p.shape[0] + route_s.shape[0]
    cp = cnt_p[0, :N_EXPERTS].astype(jnp.int32)
    cs = cnt_s[0, :N_EXPERTS].astype(jnp.int32)
    counts = cp + cs
    padded = (counts + mb - 1) // mb * mb
    pend = jnp.cumsum(padded)
    pstart = pend - padded
    nblk = -(-n_tok * TOP_K // mb) + N_EXPERTS
    first_slot = jnp.arange(nblk, dtype=jnp.int32) * mb
    blk_expert = jnp.clip(jnp.sum(pend[None, :] <= first_slot[:, None], axis=1), 0, N_EXPERTS - 1).astype(jnp.int32)
    nused = (pend[-1:] // mb).astype(jnp.int32)

    def dest(route, first):
        e = route[:, _ROUTE_E:_ROUTE_E + TOP_K].astype(jnp.int32)
        rank = route[:, _ROUTE_RANK:_ROUTE_RANK + TOP_K].astype(jnp.int32)
        onehot = e[..., None] == jnp.arange(N_EXPERTS, dtype=jnp.int32)
        return jnp.sum(jnp.where(onehot, first, 0), axis=-1) + rank

    return dest(route_p, pstart), dest(route_s, pstart + cp), blk_expert, nused, nblk


def _page_spec(shape, n_pages, pp, j):
    return pl.BlockSpec((1,) + shape, lambda b, c, pt: (pt[b * n_pages + c * pp + j],) + (0,) * len(shape))


def _slope_col():
    h = lax.broadcasted_iota(jnp.int32, (N_NSA_HEADS, 1), 0).astype(F32)
    return jnp.exp2(-8.0 * (h + 1.0) / N_NSA_HEADS)


def _rounded(ref):
    return ref[0].astype(BF16).astype(F32)


def _append_key(q, s_bias, k_new, v_new, m_ref, l_ref, acc_ref):
    s = jnp.sum(q.astype(F32) * k_new, axis=1, keepdims=True) * LN2 + s_bias
    m_prev = m_ref[0]
    m_new = jnp.maximum(m_prev, s)
    alpha = jnp.exp(m_prev - m_new)
    p = jnp.exp(s - m_new)
    l_ref[0] = alpha * l_ref[0] + p
    acc_ref[0] = alpha * acc_ref[0] + p * v_new
    m_ref[0] = m_new


def _rows(h):
    return slice(h * HEAD_DIM, (h + 1) * HEAD_DIM)


def _page_scores(q, k, group):
    return jnp.concatenate([jnp.sum(k[_rows(h // group)] * q[_rows(h)], axis=0, keepdims=True)
                            for h in range(q.shape[0] // HEAD_DIM)], axis=0)


def _paged_update(s, v_refs, group, m_ref, l_ref, acc_ref):
    pp = len(v_refs)
    m_prev = m_ref[...]
    m_new = jnp.maximum(m_prev, jnp.max(s, axis=1, keepdims=True))
    alpha = jnp.exp2(m_prev - m_new)
    p = jnp.exp2(s - jnp.tile(m_new, (1, pp)))
    l_ref[...] = alpha * l_ref[...] + sum(p[:, j * LANES:(j + 1) * LANES] for j in range(pp))
    for h in range(s.shape[0]):
        upd = acc_ref[_rows(h), :] * alpha[h:h + 1, :]
        for j in range(pp):
            upd = upd + p[h:h + 1, j * LANES:(j + 1) * LANES] * v_refs[j][0, _rows(h // group), :]
        acc_ref[_rows(h), :] = upd
    m_ref[...] = m_new


def _paged_finish(q, s_bias, k_new, v_new, group, m_ref, l_ref, acc_ref, o_ref):
    s_new = _page_scores(q, k_new, group) + s_bias
    m_prev = m_ref[...]
    m_new = jnp.maximum(m_prev, s_new)
    alpha = jnp.exp2(m_prev - m_new)
    first = lax.broadcasted_iota(jnp.int32, (1, LANES), 1) == 0
    p_new = jnp.where(first, jnp.exp2(s_new - m_new), 0.0)
    l = alpha * l_ref[...] + p_new
    for h in range(s_new.shape[0]):
        acc = acc_ref[_rows(h), :] * alpha[h:h + 1, :] + p_new[h:h + 1, :] * v_new[_rows(h // group)]
        o = jnp.sum(acc, axis=1, keepdims=True) / jnp.sum(l[h:h + 1, :], axis=1, keepdims=True)
        o_ref[0, _rows(h), :] = jnp.broadcast_to(o, (HEAD_DIM, LANES))


def _fox_sample_kernel(pt_ref, q_ref, kn_ref, vn_ref, *refs, pp):
    del pt_ref
    k_refs, v_refs = refs[:pp], refs[pp:2 * pp]
    c_ref, cn_ref, o_ref, m_ref, l_ref, acc_ref = refs[2 * pp:]
    c = pl.program_id(1)

    @pl.when(c == 0)
    def _():
        _init_softmax(m_ref, l_ref, acc_ref)

    q = q_ref[0]
    s = jnp.concatenate([_page_scores(q, r[0], 1) for r in k_refs], axis=1) - c_ref[0]
    _paged_update(s, v_refs, 1, m_ref, l_ref, acc_ref)

    @pl.when(c == pl.num_programs(1) - 1)
    def _():
        _paged_finish(q, -cn_ref[0], kn_ref[0], vn_ref[0], 1, m_ref, l_ref, acc_ref, o_ref)


def _paged_call(kernel, name, pt, head_inputs, cache_k, cache_v, extra, extra_specs, pp):
    b, n_pages = pt.shape
    rows = cache_k.shape[1]
    bmap = lambda b, c, pt: (b, 0, 0)
    per_b = lambda a: pl.BlockSpec((1,) + a.shape[1:], bmap)
    out_rows = head_inputs[0].shape[1]
    grid_spec = pltpu.PrefetchScalarGridSpec(
        num_scalar_prefetch=1, grid=(b, n_pages // pp),
        in_specs=([per_b(a) for a in head_inputs]
                  + [_page_spec((rows, PAGE_SIZE), n_pages, pp, j) for j in range(pp)] * 2 + extra_specs),
        out_specs=pl.BlockSpec((1, out_rows, LANES), bmap),
        scratch_shapes=[pltpu.VMEM((out_rows // HEAD_DIM, LANES), F32), pltpu.VMEM((out_rows // HEAD_DIM, LANES), F32),
                        pltpu.VMEM((out_rows, LANES), F32)],
    )
    return pl.pallas_call(
        kernel, grid_spec=grid_spec, out_shape=jax.ShapeDtypeStruct((b, out_rows, LANES), F32),
        compiler_params=_cparams(("parallel", "arbitrary")), name=name,
    )(pt.reshape(-1), *head_inputs, *([cache_k] * pp), *([cache_v] * pp), *extra)


def _fox_sample(pt, q_rep, kn_rep, vn_rep, cache_kt, cache_vt, cs, c_new, pp):
    bmap = lambda b, c, pt: (b, 0, 0)
    specs = [pl.BlockSpec((1, N_FOX_HEADS, pp * PAGE_SIZE), lambda b, c, pt: (b, 0, c)),
             pl.BlockSpec((1, N_FOX_HEADS, LANES), bmap)]
    return _paged_call(functools.partial(_fox_sample_kernel, pp=pp), "fox_sample", pt, [q_rep, kn_rep, vn_rep],
                       cache_kt, cache_vt, [cs, c_new], specs, pp)


def _cmp_sample_kernel(pt_ref, q_ref, *refs, pp, past):
    del pt_ref
    xk_refs, xv_refs = refs[:pp], refs[pp:2 * pp]
    w_ref, pet_ref, scat_ref, g_ref, hm_ref, ov_ref, oc_ref, sel_ref, a_ref = refs[2 * pp:]
    c = pl.program_id(1)
    rpp = PAGE_SIZE // CMP_STRIDE
    scat = scat_ref[...]

    def block_rows(page_refs, i):
        out = []
        for r in page_refs:
            t = _dot_nt(scat, (r[0] + pet_ref[i]).astype(BF16))
            out.append(jnp.concatenate([t[o * rpp:(o + 1) * rpp] for o in range(CMP_STRIDE)], axis=1))
        return jnp.concatenate(out, axis=0).astype(BF16)

    rows = pl.ds(pl.multiple_of(c * (rpp * pp), rpp * pp), rpp * pp)
    for i, page_refs in ((0, xk_refs), (1, xk_refs), (2, xv_refs), (3, xv_refs)):
        a_ref[i, rows, :] = _dot(block_rows(page_refs, i), w_ref[i])

    @pl.when(c == pl.num_programs(1) - 1)
    def _():
        n = a_ref.shape[1]
        nsbp = ov_ref.shape[1]
        kc = _head_rms(a_ref[0] + pltpu.roll(a_ref[1], n - 1, axis=0), g_ref[...], hm_ref[...]).astype(BF16)
        vc = (a_ref[2] + pltpu.roll(a_ref[3], n - 1, axis=0)).astype(BF16)
        nc_valid = (past + 1 - CMP_BLOCK) // CMP_STRIDE + 1
        idx = lax.broadcasted_iota(jnp.int32, (N_NSA_HEADS, n), 1)
        cend = jnp.where(idx < nc_valid, idx * CMP_STRIDE + CMP_BLOCK - 1, jnp.int32(2 ** 30))
        p, o = _cmp_attend(q_ref[0], kc, vc, (past - cend).astype(F32), _slope_col())
        oc_ref[0] = o
        psum = jnp.concatenate([jnp.sum(p[0:NSA_GROUP], axis=0, keepdims=True),
                                jnp.sum(p[NSA_GROUP:2 * NSA_GROUP], axis=0, keepdims=True),
                                jnp.zeros((N_NSA_HEADS - N_NSA_KV, n), F32)], axis=0)
        t = jnp.full((N_NSA_HEADS, 1), past, jnp.int32)
        sel = _select_blocks(_dot_x2(psum, ov_ref[...]), t, nsbp)
        sel_ref[0] = jnp.concatenate([jnp.broadcast_to(sel[0:1], (NSA_GROUP, nsbp)),
                                      jnp.broadcast_to(sel[1:2], (NSA_GROUP, nsbp))], axis=0).astype(BF16)


def _cmp_sample(pt, qg, cache_ckt, cache_cvt, w4, pe4, g_k_cmp2, hm, pp):
    b, n_pages = pt.shape
    past = n_pages * PAGE_SIZE
    rpp = PAGE_SIZE // CMP_STRIDE
    n = n_pages * rpp
    nsb = -(-(past + 1) // SEL_BLOCK)
    nsbp = -(-nsb // LANES) * LANES
    ov = _overlap_matrix(n, nsbp)
    pet = jnp.tile(jnp.transpose(pe4.reshape(4, CMP_STRIDE, LANES), (0, 2, 1)), (1, 1, rpp))
    pos = np.arange(PAGE_SIZE)
    scat = np.zeros((PAGE_SIZE, PAGE_SIZE), np.float32)
    scat[(pos % CMP_STRIDE) * rpp + pos // CMP_STRIDE, pos] = 1.0
    scat = jnp.asarray(scat, BF16)
    bmap = lambda b, c, pt: (b, 0, 0)
    full = lambda a: pl.BlockSpec(a.shape, lambda b, c, pt: (0,) * a.ndim)
    grid_spec = pltpu.PrefetchScalarGridSpec(
        num_scalar_prefetch=1, grid=(b, n_pages // pp),
        in_specs=([pl.BlockSpec((1, N_NSA_HEADS, LANES), bmap)]
                  + [_page_spec((LANES, PAGE_SIZE), n_pages, pp, j) for j in range(pp)] * 2
                  + [full(w4), full(pet), full(scat), full(g_k_cmp2), full(hm), full(ov)]),
        out_specs=[pl.BlockSpec((1, N_NSA_HEADS, LANES), bmap), pl.BlockSpec((1, N_NSA_HEADS, nsbp), bmap)],
        scratch_shapes=[pltpu.VMEM((4, n, LANES), F32)],
    )
    return pl.pallas_call(
        functools.partial(_cmp_sample_kernel, pp=pp, past=past), grid_spec=grid_spec,
        out_shape=[jax.ShapeDtypeStruct((b, N_NSA_HEADS, LANES), F32),
                   jax.ShapeDtypeStruct((b, N_NSA_HEADS, nsbp), BF16)],
        compiler_params=_cparams(("parallel", "arbitrary")), name="cmp_sample",
    )(pt.reshape(-1), qg, *([cache_ckt] * pp), *([cache_cvt] * pp), w4, pet, scat, g_k_cmp2, hm, ov)


def _sel_sample_kernel(pt_ref, q_ref, kn_ref, vn_ref, *refs, pp, past):
    del pt_ref
    k_refs, v_refs = refs[:pp], refs[pp:2 * pp]
    sel_ref, o_ref, m_ref, l_ref, acc_ref = refs[2 * pp:]
    c = pl.program_id(1)
    tc = pp * PAGE_SIZE
    nsbp = sel_ref.shape[2]

    @pl.when(c == 0)
    def _():
        _init_softmax(m_ref, l_ref, acc_ref)

    q = q_ref[0]
    col = lax.broadcasted_iota(jnp.int32, (nsbp, tc), 1)
    blk = lax.broadcasted_iota(jnp.int32, (nsbp, tc), 0)
    expand = (blk == c * (tc // SEL_BLOCK) + col // SEL_BLOCK).astype(BF16)
    chosen = _dot(sel_ref[0], expand) > 0.5
    kpos = (c * tc - past + lax.broadcasted_iota(jnp.int32, (1, tc), 1)).astype(F32)
    s = jnp.concatenate([_page_scores(q, r[0], NSA_GROUP) for r in k_refs], axis=1) + (_slope_col() * LOG2E) * kpos
    _paged_update(jnp.where(chosen, s, NEG), v_refs, NSA_GROUP, m_ref, l_ref, acc_ref)

    @pl.when(c == pl.num_programs(1) - 1)
    def _():
        _paged_finish(q, 0.0, kn_ref[0], vn_ref[0], NSA_GROUP, m_ref, l_ref, acc_ref, o_ref)


def _sel_sample(pt, q_rep, kn_rep, vn_rep, cache_kt, cache_vt, selh, pp):
    past = pt.shape[1] * PAGE_SIZE
    specs = [pl.BlockSpec((1,) + selh.shape[1:], lambda b, c, pt: (b, 0, 0))]
    return _paged_call(functools.partial(_sel_sample_kernel, pp=pp, past=past), "sel_sample", pt,
                       [q_rep, kn_rep, vn_rep], cache_kt, cache_vt, [selh], specs, pp)


def _gate_sample_kernel(q_ref, os_ref, stk_ref, stv_ref, wkn_ref, wvn_ref, oc_ref, gate_ref, o_ref):
    q = q_ref[0]
    slope = _slope_col()
    nbuf = stk_ref.shape[1]
    dwin = (nbuf - lax.broadcasted_iota(jnp.int32, (1, nbuf), 1)).astype(F32)
    s_w = _dot_nt(q, stk_ref[0].astype(BF16)) * LN2 - slope * dwin
    s_n = jnp.sum(q.astype(F32) * _rounded(wkn_ref), axis=1, keepdims=True) * LN2
    mw = jnp.maximum(jnp.max(s_w, axis=1, keepdims=True), s_n)
    e = jnp.exp(s_w - mw)
    e_n = jnp.exp(s_n - mw)
    o_w = ((_dot(e.astype(BF16), stv_ref[0].astype(BF16)) + e_n * _rounded(wvn_ref))
           / (jnp.sum(e, axis=1, keepdims=True) + e_n))
    g = gate_ref[0]
    o_ref[0] = g[:, 0:1] * oc_ref[0] + g[:, 1:2] * os_ref[0] + g[:, 2:3] * o_w


def _gate_sample(qg, o_sel, st_k, st_v, wk_new, wv_new, oc, gates):
    b = qg.shape[0]
    spec = lambda a: pl.BlockSpec((1,) + a.shape[1:], lambda i: (i, 0, 0))
    args = (qg, o_sel, st_k, st_v, wk_new, wv_new, oc, gates)
    return pl.pallas_call(
        _gate_sample_kernel, grid=(b,), in_specs=[spec(a) for a in args], out_specs=spec(oc),
        out_shape=jax.ShapeDtypeStruct(oc.shape, F32),
        compiler_params=_cparams(("parallel",)), name="gate_sample",
    )(*args)


PROMPT_ROWS = 256
ATTN_TILE = 1024
CMP_Q_TILE = 1024
MOE_ROWS = 256
WEIGHT_ROWS = 512
PAGES_PER_STEP = 8


def _layer(xp, xs, c_fox_k, c_fox_v, c_fox_logf, c_cmp_k, c_cmp_v, c_sel_k, c_sel_v, st_win_k, st_win_v, page_table,
           g_attn, w_in, b_fox_f, b_nsa_gate, g_q_fox, g_k_fox, g_q_nsa, g_k_sel, g_k_win, g_k_cmp,
           pe_cmp_k, pe_cmp_v, w_cmp_k, w_cmp_v, w_out, g_ffn, w_router, b_router, w_gate_up, b_gate_up,
           w_down, b_down):
    bp, sp, d = xp.shape
    bs = xs.shape[0]
    n_pool = c_fox_k.shape[0]
    n_pages = page_table.shape[1]
    past = n_pages * PAGE_SIZE

    w, bias = _proj_weights(w_in, b_fox_f, b_nsa_gate)
    tile2 = lambda g: jnp.tile(g, 2)
    gains = jnp.stack([tile2(g_q_fox), tile2(g_q_nsa), tile2(g_k_fox), tile2(g_k_sel), tile2(g_k_win)])
    hm = _head_mean_matrix()
    wk2, pek = _cmp_weights(w_cmp_k, pe_cmp_k)
    wv2, pev = _cmp_weights(w_cmp_v, pe_cmp_v)
    w4 = jnp.concatenate([wk2, wv2])
    pe4 = jnp.concatenate([pek, pev])
    g_cmp2 = tile2(g_k_cmp).reshape(1, LANES)
    wo = w_out.astype(BF16)
    wrh, wrl, br = _router_weights(w_router, b_router)
    wg, wu = _split_gate_up(w_gate_up, WEIGHT_ROWS)
    bg = b_gate_up[:, None, 0::2]
    bu = b_gate_up[:, None, 1::2]
    wd = w_down.astype(BF16)
    bd = b_down[:, None, :]

    pr = _project(xp.reshape(bp * sp, d), g_attn, w, bias, gains, hm, PROMPT_ROWS)
    fq, nq, fk, fv, ck, cv, sk, sv, wk, wv, misc = [a.reshape(bp, sp, -1) for a in pr]
    logf = misc[:, :, :N_FOX_HEADS]
    mix_fox = _fox_prompt(fq, fk, fv, _fox_bias(misc), ATTN_TILE)
    kc, vc = _cmp_prompt(ck, cv, w4, pe4, g_cmp2, hm)
    oc, unsel = _nsa_cmp_prompt(nq, kc, vc, CMP_Q_TILE)
    mix_nsa = _nsa_mix_prompt(nq, sk, sv, unsel, wk, wv, oc, misc, ATTN_TILE)
    x1p, hp, route_p, cnt_p = _tail(xp.reshape(bp * sp, d), mix_fox.reshape(bp * sp, -1),
                                    mix_nsa.reshape(bp * sp, -1), wo, g_ffn, wrh, wrl, br, PROMPT_ROWS)

    sr = _project(xs.reshape(bs, d), g_attn, w, bias, gains, hm, bs)
    fq_s, nq_s, fk_s, fv_s, ck_s, cv_s, sk_s, sv_s, wk_s, wv_s, misc_s = sr
    logf_s = misc_s[:, :N_FOX_HEADS]
    gates_s = misc_s[:, N_FOX_HEADS:N_FOX_HEADS + 3 * N_NSA_HEADS].reshape(bs, N_NSA_HEADS, 3)
    q_heads = fq_s.reshape(bs, N_FOX_HEADS, 2, HEAD_DIM)
    q_heads = jnp.stack([q_heads[:, h, h % 2] for h in range(N_FOX_HEADS)], axis=1)
    nq_heads = nq_s.reshape(bs, N_NSA_HEADS, N_NSA_KV, HEAD_DIM)
    in_group = (jnp.arange(N_NSA_HEADS)[:, None] // NSA_GROUP == jnp.arange(N_NSA_KV)[None, :])[None, :, :, None]
    qg = jnp.where(in_group, nq_heads, jnp.zeros_like(nq_heads)).reshape(bs, N_NSA_HEADS, LANES)
    nq_heads = jnp.stack([nq_heads[:, h, h // NSA_GROUP] for h in range(N_NSA_HEADS)], axis=1)
    lane_rep = lambda a: jnp.broadcast_to(a.astype(F32).reshape(bs, -1, 1), (bs, a.size // bs, LANES))
    pages_t = lambda cache: jnp.transpose(cache, (0, 2, 3, 1)).reshape(n_pool, -1, PAGE_SIZE)
    logf_past = c_fox_logf[page_table].reshape(bs, past, N_FOX_HEADS).astype(F32)
    ct_s = _cumsum_lanes(jnp.transpose(logf_past, (0, 2, 1)))
    c_new = jnp.broadcast_to(((ct_s[:, :, -1] + logf_s) * LOG2E)[:, :, None], (bs, N_FOX_HEADS, LANES))
    row3 = lambda a: a.reshape(bs, 1, -1)
    o_fox_s = _fox_sample(page_table, lane_rep(q_heads), lane_rep(fk_s), lane_rep(fv_s), pages_t(c_fox_k),
                          pages_t(c_fox_v), ct_s * LOG2E, c_new, PAGES_PER_STEP)
    oc_s, selh = _cmp_sample(page_table, qg, pages_t(c_cmp_k), pages_t(c_cmp_v), w4, pe4, g_cmp2, hm,
                             PAGES_PER_STEP)
    o_sel_s = _sel_sample(page_table, lane_rep(nq_heads), lane_rep(sk_s), lane_rep(sv_s), pages_t(c_sel_k),
                          pages_t(c_sel_v), selh, PAGES_PER_STEP)
    o_sel_s = o_sel_s[:, :, 0].reshape(bs, N_NSA_HEADS, 1, HEAD_DIM)
    o_sel_s = jnp.where(in_group, o_sel_s, 0.0).reshape(bs, N_NSA_HEADS, LANES)
    nbuf = st_win_k.shape[1]
    o_nsa_s = _gate_sample(qg, o_sel_s, st_win_k.reshape(bs, nbuf, -1), st_win_v.reshape(bs, nbuf, -1),
                           row3(wk_s), row3(wv_s), oc_s, gates_s)
    mf_s = o_fox_s[:, :, 0].astype(BF16)
    o_nsa_s = o_nsa_s.reshape(bs, N_NSA_HEADS, N_NSA_KV, HEAD_DIM)
    mn_s = jnp.stack([o_nsa_s[:, h, h // NSA_GROUP] for h in range(N_NSA_HEADS)], axis=1).reshape(bs, -1).astype(BF16)
    x1s, hs, route_s, cnt_s = _tail(xs.reshape(bs, d), mf_s, mn_s, wo, g_ffn, wrh, wrl, br, bs)

    dest_p, dest_s, blk_expert, nused, nblk = _moe_plan(route_p, cnt_p, route_s, cnt_s, MOE_ROWS)
    slots = jnp.zeros((nblk * MOE_ROWS, d // 2), jnp.uint32)
    slots = _dispatch(hp, dest_p, slots, PROMPT_ROWS)
    slots = _dispatch(hs, dest_s, slots, bs)
    ys = _experts(slots, blk_expert, nused, wg, wu, bg, bu, wd, bd, MOE_ROWS)
    yp = _combine(x1p, route_p, dest_p, ys, PROMPT_ROWS).reshape(bp, sp, d)
    y_s = _combine(x1s, route_s, dest_s, ys, bs).reshape(bs, 1, d)

    heads = lambda a, b_, s_: a.reshape(b_, s_, -1, HEAD_DIM)
    wp = min(WINDOW, sp)
    rows_p = (heads(fk, bp, sp), heads(fv, bp, sp), logf, heads(ck, bp, sp), heads(cv, bp, sp), heads(sk, bp, sp),
              heads(sv, bp, sp), heads(wk, bp, sp)[:, -wp:], heads(wv, bp, sp)[:, -wp:])
    win_k_s = jnp.concatenate([st_win_k, heads(wk_s, bs, 1)], axis=1)[:, -nbuf:]
    win_v_s = jnp.concatenate([st_win_v, heads(wv_s, bs, 1)], axis=1)[:, -nbuf:]
    rows_s = (heads(fk_s, bs, 1), heads(fv_s, bs, 1), logf_s.reshape(bs, 1, -1), heads(ck_s, bs, 1),
              heads(cv_s, bs, 1), heads(sk_s, bs, 1), heads(sv_s, bs, 1), win_k_s, win_v_s)
    return yp, y_s, rows_p, rows_s


def kernel(x_prompt, x_sample, cache_fox_k, cache_fox_v, cache_fox_logf, cache_cmp_k, cache_cmp_v, cache_sel_k,
           cache_sel_v, state_win_k, state_win_v, page_table, g_attn, w_in, b_fox_f, b_nsa_gate, g_q_fox, g_k_fox,
           g_q_nsa, g_k_sel, g_k_win, g_k_cmp, pe_cmp_k, pe_cmp_v, w_cmp_k, w_cmp_v, w_out, g_ffn, w_router,
           b_router, w_gate_up, b_gate_up, w_down, b_down):
    depth = w_in.shape[0]
    assert depth == 1, "the two groups are threaded through a single layer"
    l = 0
    yp, y_s, rows_p, rows_s = _layer(
        x_prompt, x_sample, cache_fox_k[l], cache_fox_v[l], cache_fox_logf[l], cache_cmp_k[l], cache_cmp_v[l],
        cache_sel_k[l], cache_sel_v[l], state_win_k[l], state_win_v[l], page_table,
        g_attn[l], w_in[l], b_fox_f[l], b_nsa_gate[l], g_q_fox[l], g_k_fox[l], g_q_nsa[l], g_k_sel[l], g_k_win[l],
        g_k_cmp[l], pe_cmp_k[l], pe_cmp_v[l], w_cmp_k[l], w_cmp_v[l], w_out[l], g_ffn[l], w_router[l], b_router[l],
        w_gate_up[l], b_gate_up[l], w_down[l], b_down[l])
    return (yp, y_s) + tuple(r[None] for r in rows_p) + tuple(r[None] for r in rows_s)
```
